```python
import math
import jax
import jax.numpy as jnp
from jax import lax
import numpy as np

D_MODEL = 2048
BATCH = 2
SEQ = 4096
DEPTH = 4

GRID_W = 64
CTX_LEN = 256
N_MIXERS = 3
EPS = 1e-6
ROPE_BASE = 10000.0
N_MOD = 6

DA_HEAD_DIM = 128
DA_HEADS = D_MODEL // (2 * DA_HEAD_DIM)
DA_WIDTH = 2 * DA_HEADS * DA_HEAD_DIM
DA_Q_BLOCK = 128

SG_CHUNK = 128
SG_GROUP_DIM = 128
SG_WIDTH = D_MODEL
SG_GROUPS = SG_WIDTH // SG_GROUP_DIM

RET_HEADS = D_MODEL // 256
RET_KEY_DIM = D_MODEL // RET_HEADS
RET_VAL_DIM = 2 * D_MODEL // RET_HEADS
RET_CHUNK = 128

FFN_HIDDEN = -((-8 * D_MODEL) // (3 * 256)) * 256

kernel_name = 'hybrid_diffattn_sgmlp_retention_dit'


def rms_norm(x, gain=None):
    xf = x.astype(jnp.float32)
    y = (xf * lax.rsqrt(jnp.mean(xf * xf, axis=-1, keepdims=True) + EPS)).astype(x.dtype)
    return y if gain is None else y * gain


def modulate(h, shift, scale):
    return h * (1.0 + scale) + shift


def axial_rope_tables(n_tokens, head_dim, dtype):
    rows = n_tokens // GRID_W
    row = jnp.broadcast_to(jnp.arange(rows)[:, None], (rows, GRID_W)).reshape(-1).astype(jnp.float32)
    col = jnp.broadcast_to(jnp.arange(GRID_W)[None, :], (rows, GRID_W)).reshape(-1).astype(jnp.float32)
    n_freq = head_dim // 4
    inv_freq = ROPE_BASE ** (-jnp.arange(n_freq, dtype=jnp.float32) / n_freq)
    ang = jnp.concatenate([row[:, None] * inv_freq, col[:, None] * inv_freq], axis=-1)
    return jnp.cos(ang).astype(dtype), jnp.sin(ang).astype(dtype)


def apply_rope(x, cos, sin):
    shape = (x.shape[1],) + (1,) * (x.ndim - 3) + (cos.shape[-1],)
    cos = cos.reshape(shape)
    sin = sin.reshape(shape)
    x1, x2 = jnp.split(x, 2, axis=-1)
    return jnp.concatenate([x1 * cos - x2 * sin, x1 * sin + x2 * cos], axis=-1)


def swiglu(h, w_gate_up, w_down):
    a, g = jnp.split(h @ w_gate_up, 2, axis=-1)
    return (jax.nn.silu(a) * g) @ w_down


def diff_attn_core(q, k, v, lam, scale):
    s = jnp.einsum('bqhmd,bkhmd->bhmqk', q, k).astype(jnp.float32) * scale
    p = jax.nn.softmax(s, axis=-1)
    p_diff = p[:, :, 0] - lam * p[:, :, 1]
    return jnp.einsum('bhqk,bkhe->bqhe', p_diff.astype(v.dtype), v)


def diff_attention_mixer(hx, hc, w_qkv, w_o, lam_vecs, subln_g, lambda_init, cos, sin, need_ctx):
    def proj(h):
        b, n, _ = h.shape
        q, k, v = jnp.split(h @ w_qkv, 3, axis=-1)
        return (q.reshape(b, n, DA_HEADS, 2, DA_HEAD_DIM),
                k.reshape(b, n, DA_HEADS, 2, DA_HEAD_DIM),
                v.reshape(b, n, DA_HEADS, 2 * DA_HEAD_DIM))

    qx, kx, vx = proj(hx)
    qc, kc, vc = proj(hc)
    qx = apply_rope(qx, cos, sin)
    kx = apply_rope(kx, cos, sin)
    lv = lam_vecs.astype(jnp.float32)
    lam = jnp.exp(jnp.sum(lv[0] * lv[1])) - jnp.exp(jnp.sum(lv[2] * lv[3])) + lambda_init
    scale = DA_HEAD_DIM ** -0.5

    k_all = jnp.concatenate([kc, kx], axis=1)
    v_all = jnp.concatenate([vc, vx], axis=1)
    b, n = qx.shape[0], qx.shape[1]
    nb = n // DA_Q_BLOCK
    qb = qx.reshape(b, nb, DA_Q_BLOCK, DA_HEADS, 2, DA_HEAD_DIM).swapaxes(0, 1)
    ox = lax.map(lambda qq: diff_attn_core(qq, k_all, v_all, lam, scale), qb)
    ox = ox.swapaxes(0, 1).reshape(b, n, DA_HEADS, 2 * DA_HEAD_DIM)

    def finish(o):
        bb, nn = o.shape[0], o.shape[1]
        o = rms_norm(o, subln_g) * (1.0 - lambda_init)
        return o.reshape(bb, nn, DA_WIDTH) @ w_o

    yx = finish(ox)
    yc = finish(diff_attn_core(qc, kc, vc, lam, scale)) if need_ctx else None
    return yx, yc


def spatial_gating_mixer(h, w_in, v_gain, w_s, b_s, w_out):
    b, n, _ = h.shape
    z = jax.nn.gelu(h @ w_in, approximate=False)
    u, v = jnp.split(z, 2, axis=-1)
    v = rms_norm(v, v_gain)
    v = v.reshape(b, n // SG_CHUNK, SG_CHUNK, SG_GROUPS, SG_GROUP_DIM)
    v = jnp.einsum('gpq,bcqgd->bcpgd', w_s, v) + b_s.T[:, :, None]
    return (u * v.reshape(b, n, SG_WIDTH)) @ w_out


def retention_chunk_scan(q, k, v, log_gamma, state0):
    b, n, h, _ = q.shape
    dv = v.shape[-1]
    nc = n // RET_CHUNK

    def chunks(t):
        return t.reshape(b, nc, RET_CHUNK, h, t.shape[-1]).transpose(1, 0, 3, 2, 4)

    pos = jnp.arange(RET_CHUNK, dtype=jnp.float32)
    dist = pos[:, None] - pos[None, :]
    decay = jnp.where(dist >= 0, jnp.exp(log_gamma[:, None, None] * jnp.maximum(dist, 0.0)), 0.0)
    xi = jnp.exp(log_gamma[:, None] * (pos + 1.0))[..., None]
    zeta = jnp.exp(log_gamma[:, None] * (RET_CHUNK - 1.0 - pos))[..., None]
    g_chunk = jnp.exp(log_gamma * RET_CHUNK)[:, None, None]

    def step(s, qkv):
        qc, kc, vc = qkv
        scores = jnp.einsum('bhqd,bhkd->bhqk', qc, kc) * decay
        o = jnp.einsum('bhqk,bhke->bhqe', scores, vc) + jnp.einsum('bhqd,bhde->bhqe', qc, s) * xi
        s = g_chunk * s + jnp.einsum('bhkd,bhke->bhde', kc * zeta, vc)
        return s, o.astype(jnp.float32)

    s_final, o = lax.scan(step, state0, (chunks(q), chunks(k), chunks(v)))
    o = o.transpose(1, 0, 3, 2, 4).reshape(b, n, h, dv)
    return o, s_final


def retention_mixer(hx, hc, w_q, w_k, w_v, w_g, w_o, decay_param, cos, sin, need_ctx):
    log_gamma = -jnp.exp(decay_param.astype(jnp.float32))

    def proj(h):
        b, n, _ = h.shape
        q = (h @ w_q).reshape(b, n, RET_HEADS, RET_KEY_DIM)
        k = (h @ w_k).reshape(b, n, RET_HEADS, RET_KEY_DIM) * (RET_KEY_DIM ** -0.5)
        v = (h @ w_v).reshape(b, n, RET_HEADS, RET_VAL_DIM)
        return q, k, v

    qx, kx, vx = proj(hx)
    qx = apply_rope(qx, cos, sin)
    kx = apply_rope(kx, cos, sin)
    qc, kc, vc = proj(hc)
    s0 = jnp.zeros((hx.shape[0], RET_HEADS, RET_KEY_DIM, RET_VAL_DIM), jnp.float32)

    def flip(t):
        return jnp.flip(t, axis=1)

    oc_f, sc_f = retention_chunk_scan(qc, kc, vc, log_gamma[0], s0)
    oc_b, sc_b = retention_chunk_scan(flip(qc), flip(kc), flip(vc), log_gamma[1], s0)
    ox_f, _ = retention_chunk_scan(qx, kx, vx, log_gamma[0], sc_f)
    ox_b, _ = retention_chunk_scan(flip(qx), flip(kx), flip(vx), log_gamma[1], sc_b)

    def finish(o, h):
        b, n, _ = h.shape
        o = rms_norm(o).astype(h.dtype).reshape(b, n, RET_HEADS * RET_VAL_DIM)
        return (jax.nn.silu(h @ w_g) * o) @ w_o

    yx = finish(ox_f + flip(ox_b), hx)
    yc = finish(oc_f + flip(oc_b), hc) if need_ctx else None
    return yx, yc


def setup_inputs(seed: int = 0) -> dict:
    key = jax.random.key(seed)
    keys = iter(jax.random.split(key, 40))

    def nrm(shape, std=1.0):
        return jax.random.normal(next(keys), shape, jnp.float32) * std

    d = D_MODEL
    n_a, n_b, n_c = [len(range(kind, DEPTH, N_MIXERS)) for kind in range(N_MIXERS)]
    gamma = 1.0 - 2.0 ** (-5.0 - np.arange(RET_HEADS))
    decay0 = jnp.asarray(np.log(-np.log(gamma)), jnp.float32)
    ret_qk = RET_HEADS * RET_KEY_DIM
    ret_v = RET_HEADS * RET_VAL_DIM
    return {
        'x': nrm((BATCH, SEQ, d)),
        'c': nrm((BATCH, d)),
        'ctx': nrm((BATCH, CTX_LEN, d)),
        'c_ctx': nrm((d,)),
        'ada_w': nrm((DEPTH, d, N_MOD * d), 0.5 * d ** -0.5),
        'ada_b': nrm((DEPTH, N_MOD * d), 0.01),
        'norm_mix_g': 1.0 + nrm((DEPTH, d), 0.02),
        'norm_ffn_g': 1.0 + nrm((DEPTH, d), 0.02),
        'ffn_w_gate_up': nrm((DEPTH, d, 2 * FFN_HIDDEN), d ** -0.5),
        'ffn_w_down': nrm((DEPTH, FFN_HIDDEN, d), FFN_HIDDEN ** -0.5),
        'da_w_qkv': nrm((n_a, d, 3 * DA_WIDTH), d ** -0.5),
        'da_w_o': nrm((n_a, DA_WIDTH, d), DA_WIDTH ** -0.5),
        'da_lambda': nrm((n_a, 4, DA_HEAD_DIM), 0.1),
        'da_subln_g': 1.0 + nrm((n_a, 2 * DA_HEAD_DIM), 0.02),
        'sg_w_in': nrm((n_b, d, 2 * SG_WIDTH), d ** -0.5),
        'sg_v_g': 1.0 + nrm((n_b, SG_WIDTH), 0.02),
        'sg_w_s': nrm((n_b, SG_GROUPS, SG_CHUNK, SG_CHUNK), SG_CHUNK ** -0.5),
        'sg_b_s': 1.0 + nrm((n_b, SG_GROUPS, SG_CHUNK), 0.02),
        'sg_w_out': nrm((n_b, SG_WIDTH, d), SG_WIDTH ** -0.5),
        'ret_w_q': nrm((n_c, d, ret_qk), d ** -0.5),
        'ret_w_k': nrm((n_c, d, ret_qk), d ** -0.5),
        'ret_w_v': nrm((n_c, d, ret_v), d ** -0.5),
        'ret_w_g': nrm((n_c, d, ret_v), d ** -0.5),
        'ret_w_o': nrm((n_c, ret_v, d), ret_v ** -0.5),
        'ret_decay': decay0 + nrm((n_c, 2, RET_HEADS), 0.05),
        'final_norm_g': 1.0 + nrm((d,), 0.02),
    }


def reference(x, c, ctx, c_ctx, ada_w, ada_b, norm_mix_g, norm_ffn_g, ffn_w_gate_up, ffn_w_down,
              da_w_qkv, da_w_o, da_lambda, da_subln_g,
              sg_w_in, sg_v_g, sg_w_s, sg_b_s, sg_w_out,
              ret_w_q, ret_w_k, ret_w_v, ret_w_g, ret_w_o, ret_decay, final_norm_g):
    n_lat = x.shape[1]
    da_cos, da_sin = axial_rope_tables(n_lat, DA_HEAD_DIM, x.dtype)
    ret_cos, ret_sin = axial_rope_tables(n_lat, RET_KEY_DIM, x.dtype)
    silu_c = jax.nn.silu(c)
    silu_cc = jax.nn.silu(c_ctx)

    for i in range(DEPTH):
        last = i == DEPTH - 1
        kind = i % N_MIXERS
        j = i // N_MIXERS
        mod_x = (silu_c @ ada_w[i] + ada_b[i])[:, None, :]
        mod_c = silu_cc @ ada_w[i] + ada_b[i]
        sx1, cx1, gx1, sx2, cx2, gx2 = jnp.split(mod_x, N_MOD, axis=-1)
        sc1, cc1, gc1, sc2, cc2, gc2 = jnp.split(mod_c, N_MOD, axis=-1)

        hx = modulate(rms_norm(x, norm_mix_g[i]), sx1, cx1)
        hc = modulate(rms_norm(ctx, norm_mix_g[i]), sc1, cc1)
        if kind == 0:
            lambda_init = 0.8 - 0.6 * math.exp(-0.3 * i)
            yx, yc = diff_attention_mixer(hx, hc, da_w_qkv[j], da_w_o[j], da_lambda[j], da_subln_g[j],
                                          lambda_init, da_cos, da_sin, not last)
        elif kind == 1:
            yx = spatial_gating_mixer(hx, sg_w_in[j], sg_v_g[j], sg_w_s[j], sg_b_s[j], sg_w_out[j])
            yc = None if last else spatial_gating_mixer(hc, sg_w_in[j], sg_v_g[j], sg_w_s[j], sg_b_s[j], sg_w_out[j])
        else:
            yx, yc = retention_mixer(hx, hc, ret_w_q[j], ret_w_k[j], ret_w_v[j], ret_w_g[j], ret_w_o[j],
                                     ret_decay[j], ret_cos, ret_sin, not last)

        x = x + gx1 * yx
        x = x + gx2 * swiglu(modulate(rms_norm(x, norm_ffn_g[i]), sx2, cx2), ffn_w_gate_up[i], ffn_w_down[i])
        if not last:
            ctx = ctx + gc1 * yc
            ctx = ctx + gc2 * swiglu(modulate(rms_norm(ctx, norm_ffn_g[i]), sc2, cc2),
                                     ffn_w_gate_up[i], ffn_w_down[i])

    return rms_norm(x, final_norm_g)
```

```python
import functools
import math

import jax
import jax.numpy as jnp
from jax import lax
from jax.experimental import pallas as pl
from jax.experimental.pallas import tpu as pltpu

D_MODEL = 2048
BATCH = 2
SEQ = 4096
DEPTH = 4
GRID_W = 64
CTX_LEN = 256
N_MIXERS = 3
EPS = 1e-6
ROPE_BASE = 10000.0
N_MOD = 6

DA_HEAD_DIM = 128
DA_HEADS = D_MODEL // (2 * DA_HEAD_DIM)
DA_WIDTH = 2 * DA_HEADS * DA_HEAD_DIM
DA_HEAD_W = 2 * DA_HEAD_DIM

SG_CHUNK = 128
SG_GROUP_DIM = 128
SG_WIDTH = D_MODEL
SG_GROUPS = SG_WIDTH // SG_GROUP_DIM

RET_HEADS = D_MODEL // 256
RET_KEY_DIM = D_MODEL // RET_HEADS
RET_VAL_DIM = 2 * D_MODEL // RET_HEADS
RET_CHUNK = 128

FFN_HIDDEN = -((-8 * D_MODEL) // (3 * 256)) * 256

ROWS_B = SEQ + CTX_LEN
ROWS = BATCH * ROWS_B
ROW_TILES_B = 4
TM = ROWS_B // ROW_TILES_B
MOD_ROWS = 8
CTX_MOD_ROW = BATCH

LANE = 128
VMEM_LIMIT_CAP = 60 * 1024 * 1024

BF16 = jnp.bfloat16
F32 = jnp.float32


def _vmem_limit(block_bytes, temp_bytes):
    return int(min(VMEM_LIMIT_CAP, 2 * block_bytes + temp_bytes + (4 << 20)))


def _nbytes(shape, dtype):
    return math.prod(shape) * jnp.dtype(dtype).itemsize


def _is_latent_rows(tile_idx, tm):
    row = lax.broadcasted_iota(jnp.int32, (tm, 1), 0) + (tile_idx % (ROWS_B // tm)) * tm
    return row < SEQ


def _silu(x):
    return x * (1.0 / (1.0 + jnp.exp(-x)))


def _ada_kernel(c_ref, w_ref, b_ref, o_ref):
    s = _silu(c_ref[...]).astype(BF16)
    acc = jnp.dot(s, w_ref[...].astype(BF16), preferred_element_type=F32)
    o_ref[...] = acc + b_ref[...]


def _ada_tables(cvec, ada_w, ada_b):
    tn = 1024
    n = N_MOD * D_MODEL
    blocks = _nbytes((D_MODEL, tn), F32) + _nbytes((MOD_ROWS, tn), F32) * 2
    return pl.pallas_call(
        _ada_kernel,
        grid=(DEPTH, n // tn),
        in_specs=[
            pl.BlockSpec((MOD_ROWS, D_MODEL), lambda l, j: (0, 0)),
            pl.BlockSpec((None, D_MODEL, tn), lambda l, j: (l, 0, j)),
            pl.BlockSpec((None, 1, tn), lambda l, j: (l, 0, j)),
        ],
        out_specs=pl.BlockSpec((None, MOD_ROWS, tn), lambda l, j: (l, 0, j)),
        out_shape=jax.ShapeDtypeStruct((DEPTH, MOD_ROWS, n), F32),
        compiler_params=pltpu.CompilerParams(
            dimension_semantics=("arbitrary", "arbitrary"),
            vmem_limit_bytes=_vmem_limit(blocks, _nbytes((D_MODEL, tn), BF16))),
        name="ada_tables",
    )(cvec, ada_w, ada_b.reshape(DEPTH, 1, n))


def _normmod_kernel(x_ref, g_ref, sb_ref, cb_ref, sc_ref, cc_ref, o_ref, *, tm):
    x = x_ref[...]
    y = x * lax.rsqrt(jnp.mean(x * x, axis=-1, keepdims=True) + EPS) * g_ref[...]
    lat = _is_latent_rows(pl.program_id(0), tm)
    shift = jnp.where(lat, sb_ref[...], sc_ref[...])
    scale = jnp.where(lat, cb_ref[...], cc_ref[...])
    o_ref[...] = (y * (1.0 + scale) + shift).astype(o_ref.dtype)


def _mod_specs(which, width, tiles_per_batch, col_of):
    def batch_map(i, *rest):
        return (i // tiles_per_batch, 0, col_of(which, i, *rest))

    def ctx_map(i, *rest):
        return (CTX_MOD_ROW, 0, col_of(which, i, *rest))

    return (pl.BlockSpec((None, 1, width), batch_map), pl.BlockSpec((None, 1, width), ctx_map))


def _normmod(xs, gain, modt, shift_idx, scale_idx):
    tm = TM // 2
    tiles_b = ROWS_B // tm
    full = lambda which, i: which
    sb, sc = _mod_specs(shift_idx, D_MODEL, tiles_b, full)
    cb, cc = _mod_specs(scale_idx, D_MODEL, tiles_b, full)
    blocks = _nbytes((tm, D_MODEL), F32) + _nbytes((tm, D_MODEL), BF16)
    return pl.pallas_call(
        functools.partial(_normmod_kernel, tm=tm),
        grid=(ROWS // tm,),
        in_specs=[pl.BlockSpec((tm, D_MODEL), lambda i: (i, 0)),
                  pl.BlockSpec((1, D_MODEL), lambda i: (0, 0)),
                  sb, cb, sc, cc],
        out_specs=pl.BlockSpec((tm, D_MODEL), lambda i: (i, 0)),
        out_shape=jax.ShapeDtypeStruct((ROWS, D_MODEL), BF16),
        compiler_params=pltpu.CompilerParams(
            dimension_semantics=("arbitrary",),
            vmem_limit_bytes=_vmem_limit(blocks, 3 * _nbytes((tm, D_MODEL), F32))),
        name="normmod",
    )(xs, gain.reshape(1, D_MODEL), modt, modt, modt, modt)


def _rope_half_roll(acc, cos2, sin2):
    return acc * cos2 + pltpu.roll(acc, DA_HEAD_DIM // 2, 1) * sin2


def _proj_kernel(*refs, epi, tn, col_scale):
    if epi in ("rope128", "rope256"):
        a_ref, w_ref, cos_ref, sin_ref, o_ref = refs
    else:
        a_ref, w_ref, o_ref = refs
    acc = jnp.dot(a_ref[...], w_ref[...].astype(BF16), preferred_element_type=F32)
    if epi == "plain":
        o_ref[...] = acc.astype(o_ref.dtype)
    elif epi == "silu":
        o_ref[...] = _silu(acc).astype(o_ref.dtype)
    elif epi == "gelu":
        o_ref[...] = (0.5 * acc * (1.0 + lax.erf(acc * (2.0 ** -0.5)))).astype(o_ref.dtype)
    elif epi == "rope256":
        cos, sin = cos_ref[...], sin_ref[...]
        for g in range(tn // RET_KEY_DIM):
            lo = g * RET_KEY_DIM
            x1 = acc[:, lo:lo + LANE]
            x2 = acc[:, lo + LANE:lo + 2 * LANE]
            o_ref[:, lo:lo + LANE] = ((x1 * cos - x2 * sin) * col_scale).astype(o_ref.dtype)
            o_ref[:, lo + LANE:lo + 2 * LANE] = ((x1 * sin + x2 * cos) * col_scale).astype(o_ref.dtype)
    elif epi == "rope128":
        j = pl.program_id(1)
        tiles_per_part = DA_WIDTH // tn

        @pl.when(j < 2 * tiles_per_part)
        def _():
            cos2, sin2 = cos_ref[...], sin_ref[...]
            scale = jnp.where(j < tiles_per_part, col_scale, 1.0)
            for g in range(tn // DA_HEAD_DIM):
                lo = g * DA_HEAD_DIM
                r = _rope_half_roll(acc[:, lo:lo + DA_HEAD_DIM], cos2, sin2)
                o_ref[:, lo:lo + DA_HEAD_DIM] = (r * scale).astype(o_ref.dtype)

        @pl.when(j >= 2 * tiles_per_part)
        def _():
            o_ref[...] = acc.astype(o_ref.dtype)
    else:
        raise ValueError(epi)


def _proj(a, w, *, epi, out_dtype, tables=None, col_scale=1.0, name):
    k, n = w.shape
    tn = 512
    in_specs = [pl.BlockSpec((TM, k), lambda i, j: (i, 0)),
                pl.BlockSpec((k, tn), lambda i, j: (0, j))]
    args = [a, w]
    if tables is not None:
        in_specs += [pl.BlockSpec((TM, LANE), lambda i, j: (i % ROW_TILES_B, 0))] * 2
        args += list(tables)
    blocks = (_nbytes((TM, k), BF16) + _nbytes((k, tn), F32) + _nbytes((TM, tn), out_dtype)
              + 2 * _nbytes((TM, LANE), F32))
    temps = _nbytes((k, tn), BF16) + 3 * _nbytes((TM, tn), F32)
    return pl.pallas_call(
        functools.partial(_proj_kernel, epi=epi, tn=tn, col_scale=col_scale),
        grid=(ROWS // TM, n // tn),
        in_specs=in_specs,
        out_specs=pl.BlockSpec((TM, tn), lambda i, j: (i, j)),
        out_shape=jax.ShapeDtypeStruct((ROWS, n), out_dtype),
        compiler_params=pltpu.CompilerParams(
            dimension_semantics=("arbitrary", "arbitrary"),
            vmem_limit_bytes=_vmem_limit(blocks, temps)),
        name=name,
    )(*args)


def _glu_kernel(a_ref, wg_ref, wu_ref, o_ref):
    a = a_ref[...]
    gate = jnp.dot(a, wg_ref[...].astype(BF16), preferred_element_type=F32)
    up = jnp.dot(a, wu_ref[...].astype(BF16), preferred_element_type=F32)
    o_ref[...] = (_silu(gate) * up).astype(o_ref.dtype)


def _glu(a, w_gate_up):
    k = a.shape[1]
    tn = 512
    nt = FFN_HIDDEN // tn
    blocks = _nbytes((TM, k), BF16) + 2 * _nbytes((k, tn), F32) + _nbytes((TM, tn), BF16)
    temps = 2 * _nbytes((k, tn), BF16) + 4 * _nbytes((TM, tn), F32)
    return pl.pallas_call(
        _glu_kernel,
        grid=(ROWS // TM, nt),
        in_specs=[pl.BlockSpec((TM, k), lambda i, j: (i, 0)),
                  pl.BlockSpec((k, tn), lambda i, j: (0, j)),
                  pl.BlockSpec((k, tn), lambda i, j: (0, j + nt))],
        out_specs=pl.BlockSpec((TM, tn), lambda i, j: (i, j)),
        out_shape=jax.ShapeDtypeStruct((ROWS, FFN_HIDDEN), BF16),
        compiler_params=pltpu.CompilerParams(
            dimension_semantics=("arbitrary", "arbitrary"),
            vmem_limit_bytes=_vmem_limit(blocks, temps)),
        name="ffn_glu",
    )(a, w_gate_up, w_gate_up)


def _resid_kernel(a_ref, w_ref, x_ref, gb_ref, gc_ref, o_ref):
    acc = jnp.dot(a_ref[...], w_ref[...].astype(BF16), preferred_element_type=F32)
    gate = jnp.where(_is_latent_rows(pl.program_id(0), TM), gb_ref[...], gc_ref[...])
    o_ref[...] = x_ref[...] + gate * acc


def _resid(a, w, xs, modt, gate_idx, *, name):
    k = w.shape[0]
    tn = 512 if k <= D_MODEL else 256
    tiles_per_vec = D_MODEL // tn
    gb, gc = _mod_specs(gate_idx, tn, ROW_TILES_B, lambda which, i, j: which * tiles_per_vec + j)
    blocks = _nbytes((TM, k), BF16) + _nbytes((k, tn), F32) + 2 * _nbytes((TM, tn), F32)
    temps = _nbytes((k, tn), BF16) + 3 * _nbytes((TM, tn), F32)
    return pl.pallas_call(
        _resid_kernel,
        grid=(ROWS // TM, D_MODEL // tn),
        in_specs=[pl.BlockSpec((TM, k), lambda i, j: (i, 0)),
                  pl.BlockSpec((k, tn), lambda i, j: (0, j)),
                  pl.BlockSpec((TM, tn), lambda i, j: (i, j)),
                  gb, gc],
        out_specs=pl.BlockSpec((TM, tn), lambda i, j: (i, j)),
        out_shape=jax.ShapeDtypeStruct((ROWS, D_MODEL), F32),
        compiler_params=pltpu.CompilerParams(
            dimension_semantics=("arbitrary", "arbitrary"),
            vmem_limit_bytes=_vmem_limit(blocks, temps)),
        name=name,
    )(a, w, xs, modt, modt)


def _diff_attn_kernel(lam_ref, q_ref, k_ref, v_ref, g_ref, o_ref, *, out_scale):
    q = q_ref[...]
    k = k_ref[...]

    def softmax_parts(m):
        lo = m * DA_HEAD_DIM
        s = lax.dot_general(q[:, lo:lo + DA_HEAD_DIM], k[:, lo:lo + DA_HEAD_DIM],
                            (((1,), (1,)), ((), ())), preferred_element_type=F32)
        p = jnp.exp(s - jnp.max(s, axis=-1, keepdims=True))
        return p, jnp.sum(p, axis=-1, keepdims=True)

    p0, l0 = softmax_parts(0)
    p1, l1 = softmax_parts(1)
    p_diff = p0 * (1.0 / l0) - p1 * (lam_ref[0] / l1)
    o = jnp.dot(p_diff.astype(BF16), v_ref[...], preferred_element_type=F32)
    o = o * lax.rsqrt(jnp.mean(o * o, axis=-1, keepdims=True) + EPS) * g_ref[...] * out_scale
    o_ref[...] = o.astype(o_ref.dtype)


def _diff_attn(qkv, lam, subln_g, out_scale, *, tq, q_tiles, q_tile0, kv_rows, kv_blk0, prev=None):
    tiles_b = ROWS_B // tq
    kcol0 = DA_WIDTH // DA_HEAD_W
    vcol0 = 2 * kcol0
    blocks = (2 * _nbytes((tq, DA_HEAD_W), BF16) + 2 * _nbytes((kv_rows, DA_HEAD_W), BF16))
    temps = 6 * _nbytes((tq, kv_rows), F32)
    in_specs = [
        pl.BlockSpec(memory_space=pltpu.SMEM),
        pl.BlockSpec((tq, DA_HEAD_W), lambda b, h, t: (b * tiles_b + q_tile0 + t, h)),
        pl.BlockSpec((kv_rows, DA_HEAD_W), lambda b, h, t: (b * (ROWS_B // kv_rows) + kv_blk0, kcol0 + h)),
        pl.BlockSpec((kv_rows, DA_HEAD_W), lambda b, h, t: (b * (ROWS_B // kv_rows) + kv_blk0, vcol0 + h)),
        pl.BlockSpec((1, DA_HEAD_W), lambda b, h, t: (0, 0)),
    ]
    args = [lam, qkv, qkv, qkv, subln_g.reshape(1, DA_HEAD_W)]
    aliases = {}
    if prev is not None:
        in_specs.append(pl.BlockSpec(memory_space=pl.ANY))
        args.append(prev)
        aliases = {5: 0}

    def body(*refs):
        if prev is not None:
            refs = refs[:5] + refs[6:]
        _diff_attn_kernel(*refs, out_scale=out_scale)

    return pl.pallas_call(
        body,
        grid=(BATCH, DA_HEADS, q_tiles),
        in_specs=in_specs,
        out_specs=pl.BlockSpec((tq, DA_HEAD_W), lambda b, h, t: (b * tiles_b + q_tile0 + t, h)),
        out_shape=jax.ShapeDtypeStruct((ROWS, DA_WIDTH), BF16),
        input_output_aliases=aliases,
        compiler_params=pltpu.CompilerParams(
            dimension_semantics=("arbitrary", "arbitrary", "arbitrary"),
            vmem_limit_bytes=_vmem_limit(blocks, temps)),
        name="diff_attn",
    )(*args)


def _sg_gate_kernel(z_ref, vg_ref, ws_ref, bs_ref, o_ref, *, chunks):
    v = z_ref[:, SG_WIDTH:]
    v = v * lax.rsqrt(jnp.mean(v * v, axis=-1, keepdims=True) + EPS) * vg_ref[...]
    for c in range(chunks):
        r0 = c * SG_CHUNK
        for g in range(SG_GROUPS):
            c0 = g * SG_GROUP_DIM
            vg = v[r0:r0 + SG_CHUNK, c0:c0 + SG_GROUP_DIM].astype(BF16)
            mixed = jnp.dot(ws_ref[g].astype(BF16), vg, preferred_element_type=F32) + bs_ref[g]
            u = z_ref[r0:r0 + SG_CHUNK, c0:c0 + SG_GROUP_DIM]
            o_ref[r0:r0 + SG_CHUNK, c0:c0 + SG_GROUP_DIM] = (u * mixed).astype(o_ref.dtype)


def _sg_gate(z, v_gain, w_s, b_s):
    chunks = 2
    tm = chunks * SG_CHUNK
    bs_b = jnp.broadcast_to(b_s[:, :, None], (SG_GROUPS, SG_CHUNK, SG_GROUP_DIM))
    blocks = (_nbytes((tm, 2 * SG_WIDTH), F32) + _nbytes((tm, SG_WIDTH), BF16)
              + 2 * _nbytes((SG_GROUPS, SG_CHUNK, SG_CHUNK), F32))
    return pl.pallas_call(
        functools.partial(_sg_gate_kernel, chunks=chunks),
        grid=(ROWS // tm,),
        in_specs=[pl.BlockSpec((tm, 2 * SG_WIDTH), lambda i: (i, 0)),
                  pl.BlockSpec((1, SG_WIDTH), lambda i: (0, 0)),
                  pl.BlockSpec((SG_GROUPS, SG_CHUNK, SG_CHUNK), lambda i: (0, 0, 0)),
                  pl.BlockSpec((SG_GROUPS, SG_CHUNK, SG_GROUP_DIM), lambda i: (0, 0, 0))],
        out_specs=pl.BlockSpec((tm, SG_WIDTH), lambda i: (i, 0)),
        out_shape=jax.ShapeDtypeStruct((ROWS, SG_WIDTH), BF16),
        compiler_params=pltpu.CompilerParams(
            dimension_semantics=("arbitrary",),
            vmem_limit_bytes=_vmem_limit(blocks, 4 * _nbytes((tm, SG_WIDTH), F32))),
        name="sg_gate",
    )(z, v_gain.reshape(1, SG_WIDTH), w_s, bs_b)


def _ret_kernel(*refs, backward):
    if backward:
        g_ref, q_ref, k_ref, v_ref, dec_ref, xi_ref, zeta_ref, of_ref, gp_ref, o_ref, s_ref = refs
    else:
        g_ref, q_ref, k_ref, v_ref, dec_ref, xi_ref, zeta_ref, o_ref, s_ref = refs

    @pl.when(pl.program_id(1) == 0)
    def _():
        s_ref[...] = jnp.zeros_like(s_ref)

    for h in range(RET_HEADS):
        kc = slice(h * RET_KEY_DIM, (h + 1) * RET_KEY_DIM)
        vc = slice(h * RET_VAL_DIM, (h + 1) * RET_VAL_DIM)
        qh = q_ref[:, kc]
        kh = k_ref[:, kc]
        vh = v_ref[:, vc]
        scores = lax.dot_general(qh, kh, (((1,), (1,)), ((), ())), preferred_element_type=F32) * dec_ref[h]
        state = s_ref[h]
        o = jnp.dot(scores.astype(BF16), vh, preferred_element_type=F32)
        o = o + jnp.dot(qh, state.astype(BF16), preferred_element_type=F32) * xi_ref[h]
        kz_t = (kh.astype(F32) * zeta_ref[h]).T.astype(BF16)
        s_ref[h] = g_ref[h] * state + jnp.dot(kz_t, vh, preferred_element_type=F32)
        if backward:
            o = o + of_ref[:, vc]
            o = o * lax.rsqrt(jnp.mean(o * o, axis=-1, keepdims=True) + EPS)
            o_ref[:, vc] = (gp_ref[:, vc] * o).astype(o_ref.dtype)
        else:
            o_ref[:, vc] = o


def _ret_consts(log_gamma, backward):
    pos = jnp.arange(RET_CHUNK, dtype=F32)
    dist = pos[:, None] - pos[None, :]
    if backward:
        dist = -dist
    lg = log_gamma[:, None, None]
    decay = jnp.where(dist >= 0, jnp.exp(lg * jnp.maximum(dist, 0.0)), 0.0)
    pos_in_scan = (RET_CHUNK - 1.0 - pos) if backward else pos
    xi = jnp.exp(log_gamma[:, None] * (pos_in_scan + 1.0))
    zeta = jnp.exp(log_gamma[:, None] * (RET_CHUNK - 1.0 - pos_in_scan))
    g_chunk = jnp.exp(log_gamma * RET_CHUNK)
    bcast = lambda t, w: jnp.broadcast_to(t[:, :, None], (RET_HEADS, RET_CHUNK, w))
    return g_chunk, decay, bcast(xi, RET_VAL_DIM), bcast(zeta, RET_KEY_DIM)


def _ret_scan(q, k, v, log_gamma, *, backward, o_fwd=None, gproj=None):
    n_chunks = ROWS_B // RET_CHUNK
    x_chunks = SEQ // RET_CHUNK
    g_chunk, decay, xi, zeta = _ret_consts(log_gamma, backward)
    if backward:
        chunk_of = lambda b, t: (b * n_chunks + (n_chunks - 1 - t), 0)
    else:
        chunk_of = lambda b, t: (b * n_chunks + (t + x_chunks) % n_chunks, 0)
    qk_w = RET_HEADS * RET_KEY_DIM
    v_w = RET_HEADS * RET_VAL_DIM
    const_spec = lambda w: pl.BlockSpec((RET_HEADS, RET_CHUNK, w), lambda b, t: (0, 0, 0))
    in_specs = [pl.BlockSpec(memory_space=pltpu.SMEM),
                pl.BlockSpec((RET_CHUNK, qk_w), chunk_of),
                pl.BlockSpec((RET_CHUNK, qk_w), chunk_of),
                pl.BlockSpec((RET_CHUNK, v_w), chunk_of),
                const_spec(RET_CHUNK), const_spec(RET_VAL_DIM), const_spec(RET_KEY_DIM)]
    args = [g_chunk, q, k, v, decay, xi, zeta]
    blocks = (2 * _nbytes((RET_CHUNK, qk_w), BF16) + _nbytes((RET_CHUNK, v_w), BF16)
              + _nbytes((RET_HEADS, RET_CHUNK, RET_CHUNK + RET_VAL_DIM + RET_KEY_DIM), F32)
              + _nbytes((RET_CHUNK, v_w), F32))
    if backward:
        in_specs += [pl.BlockSpec((RET_CHUNK, v_w), chunk_of), pl.BlockSpec((RET_CHUNK, v_w), chunk_of)]
        args += [o_fwd, gproj]
        blocks += 2 * _nbytes((RET_CHUNK, v_w), F32)
        out_dtype = BF16
    else:
        out_dtype = F32
    state_bytes = _nbytes((RET_HEADS, RET_KEY_DIM, RET_VAL_DIM), F32)
    return pl.pallas_call(
        functools.partial(_ret_kernel, backward=backward),
        grid=(BATCH, n_chunks),
        in_specs=in_specs,
        out_specs=pl.BlockSpec((RET_CHUNK, v_w), chunk_of),
        out_shape=jax.ShapeDtypeStruct((ROWS, v_w), out_dtype),
        scratch_shapes=[pltpu.VMEM((RET_HEADS, RET_KEY_DIM, RET_VAL_DIM), F32)],
        compiler_params=pltpu.CompilerParams(
            dimension_semantics=("arbitrary", "arbitrary"),
            vmem_limit_bytes=_vmem_limit(blocks, state_bytes + (8 << 20))),
        name="ret_bwd" if backward else "ret_fwd",
    )(*args)


def _final_norm_kernel(x_ref, g_ref, o_ref):
    x = x_ref[...]
    o_ref[...] = x * lax.rsqrt(jnp.mean(x * x, axis=-1, keepdims=True) + EPS) * g_ref[...]


def _final_norm(xs, gain):
    tr = CTX_LEN
    blocks = 2 * _nbytes((tr, D_MODEL), F32)
    return pl.pallas_call(
        _final_norm_kernel,
        grid=(BATCH, SEQ // tr),
        in_specs=[pl.BlockSpec((tr, D_MODEL), lambda b, t: (b * (ROWS_B // tr) + t, 0)),
                  pl.BlockSpec((1, D_MODEL), lambda b, t: (0, 0))],
        out_specs=pl.BlockSpec((None, tr, D_MODEL), lambda b, t: (b, t, 0)),
        out_shape=jax.ShapeDtypeStruct((BATCH, SEQ, D_MODEL), F32),
        compiler_params=pltpu.CompilerParams(
            dimension_semantics=("arbitrary", "arbitrary"),
            vmem_limit_bytes=_vmem_limit(blocks, 2 * _nbytes((tr, D_MODEL), F32))),
        name="final_norm",
    )(xs, gain.reshape(1, D_MODEL))


def _rope_tables(head_dim):
    rows = SEQ // GRID_W
    row = jnp.broadcast_to(jnp.arange(rows)[:, None], (rows, GRID_W)).reshape(-1).astype(F32)
    col = jnp.broadcast_to(jnp.arange(GRID_W)[None, :], (rows, GRID_W)).reshape(-1).astype(F32)
    n_freq = head_dim // 4
    inv_freq = ROPE_BASE ** (-jnp.arange(n_freq, dtype=F32) / n_freq)
    ang = jnp.concatenate([row[:, None] * inv_freq, col[:, None] * inv_freq], axis=-1)
    cos = jnp.concatenate([jnp.cos(ang), jnp.ones((CTX_LEN, head_dim // 2), F32)], axis=0)
    sin = jnp.concatenate([jnp.sin(ang), jnp.zeros((CTX_LEN, head_dim // 2), F32)], axis=0)
    return cos, sin


def kernel(x, c, ctx, c_ctx, ada_w, ada_b, norm_mix_g, norm_ffn_g, ffn_w_gate_up, ffn_w_down,
           da_w_qkv, da_w_o, da_lambda, da_subln_g,
           sg_w_in, sg_v_g, sg_w_s, sg_b_s, sg_w_out,
           ret_w_q, ret_w_k, ret_w_v, ret_w_g, ret_w_o, ret_decay, final_norm_g):
    xs = jnp.concatenate([x, ctx], axis=1).reshape(ROWS, D_MODEL)
    cvec = jnp.concatenate([c, c_ctx[None, :], jnp.zeros((MOD_ROWS - BATCH - 1, D_MODEL), F32)], axis=0)
    mods = _ada_tables(cvec, ada_w, ada_b)

    da_cos, da_sin = _rope_tables(DA_HEAD_DIM)
    da_tables = (jnp.concatenate([da_cos, da_cos], axis=1), jnp.concatenate([-da_sin, da_sin], axis=1))
    ret_tables = _rope_tables(RET_KEY_DIM)

    for i in range(DEPTH):
        last = i == DEPTH - 1
        kind = i % N_MIXERS
        j = i // N_MIXERS
        modt = mods[i].reshape(MOD_ROWS, 1, N_MOD * D_MODEL)
        h = _normmod(xs, norm_mix_g[i], modt, 0, 1)
        if kind == 0:
            lambda_init = 0.8 - 0.6 * math.exp(-0.3 * i)
            lv = da_lambda[j].astype(F32)
            lam = (jnp.exp(jnp.sum(lv[0] * lv[1])) - jnp.exp(jnp.sum(lv[2] * lv[3])) + lambda_init).reshape(1)
            qkv = _proj(h, da_w_qkv[j], epi="rope128", out_dtype=BF16, tables=da_tables,
                        col_scale=DA_HEAD_DIM ** -0.5, name="da_qkv")
            attn = functools.partial(_diff_attn, qkv, lam, da_subln_g[j], 1.0 - lambda_init)
            y = attn(tq=CTX_LEN, q_tiles=SEQ // CTX_LEN, q_tile0=0, kv_rows=ROWS_B, kv_blk0=0)
            y = attn(tq=CTX_LEN, q_tiles=1, q_tile0=SEQ // CTX_LEN, kv_rows=CTX_LEN,
                     kv_blk0=SEQ // CTX_LEN, prev=y)
            xs = _resid(y, da_w_o[j], xs, modt, 2, name="da_out")
        elif kind == 1:
            z = _proj(h, sg_w_in[j], epi="gelu", out_dtype=F32, name="sg_in")
            y = _sg_gate(z, sg_v_g[j], sg_w_s[j], sg_b_s[j])
            xs = _resid(y, sg_w_out[j], xs, modt, 2, name="sg_out")
        else:
            log_gamma = -jnp.exp(ret_decay[j].astype(F32))
            q = _proj(h, ret_w_q[j], epi="rope256", out_dtype=BF16, tables=ret_tables, name="ret_q")
            k = _proj(h, ret_w_k[j], epi="rope256", out_dtype=BF16, tables=ret_tables,
                      col_scale=RET_KEY_DIM ** -0.5, name="ret_k")
            v = _proj(h, ret_w_v[j], epi="plain", out_dtype=BF16, name="ret_v")
            gp = _proj(h, ret_w_g[j], epi="silu", out_dtype=F32, name="ret_g")
            o_f = _ret_scan(q, k, v, log_gamma[0], backward=False)
            y = _ret_scan(q, k, v, log_gamma[1], backward=True, o_fwd=o_f, gproj=gp)
            xs = _resid(y, ret_w_o[j], xs, modt, 2, name="ret_out")
        h = _normmod(xs, norm_ffn_g[i], modt, 3, 4)
        act = _glu(h, ffn_w_gate_up[i])
        xs = _resid(act, ffn_w_down[i], xs, modt, 5, name="ffn_down")
    return _final_norm(xs, final_norm_g)
```

```python
import functools
import math

import jax
import jax.numpy as jnp
from jax import lax
from jax.experimental import pallas as pl
from jax.experimental.pallas import tpu as pltpu

D_MODEL = 2048
BATCH = 2
SEQ = 4096
DEPTH = 4
GRID_W = 64
CTX_LEN = 256
N_MIXERS = 3
EPS = 1e-6
ROPE_BASE = 10000.0
N_MOD = 6

DA_HEAD_DIM = 128
DA_HEADS = D_MODEL // (2 * DA_HEAD_DIM)
DA_WIDTH = 2 * DA_HEADS * DA_HEAD_DIM
DA_HEAD_W = 2 * DA_HEAD_DIM

SG_CHUNK = 128
SG_GROUP_DIM = 128
SG_WIDTH = D_MODEL
SG_GROUPS = SG_WIDTH // SG_GROUP_DIM

RET_HEADS = D_MODEL // 256
RET_KEY_DIM = D_MODEL // RET_HEADS
RET_VAL_DIM = 2 * D_MODEL // RET_HEADS
RET_CHUNK = 128

FFN_HIDDEN = -((-8 * D_MODEL) // (3 * 256)) * 256

ROWS_B = SEQ + CTX_LEN
ROWS = BATCH * ROWS_B
TM_WIDE = ROWS_B // 2
TM_DEEP = ROWS_B // 4
NORM_TM = CTX_LEN
NORM_STRIP = 16
MOD_ROWS = 8
CTX_MOD_ROW = BATCH

LANE = 128
SUBLANE = 8
MXU_DIM = 256
VMEM_LIMIT_CAP = 60 * 1024 * 1024

BF16 = jnp.bfloat16
F32 = jnp.float32


def _vmem_limit(block_bytes, temp_bytes):
    return int(min(VMEM_LIMIT_CAP, 2 * block_bytes + temp_bytes + (4 << 20)))


def _nbytes(shape, dtype):
    return math.prod(shape) * jnp.dtype(dtype).itemsize


def _is_latent_rows(tile_idx, tm):
    row = lax.broadcasted_iota(jnp.int32, (tm, 1), 0) + (tile_idx % (ROWS_B // tm)) * tm
    return row < SEQ


def _silu(x):
    return x * (1.0 / (1.0 + jnp.exp(-x)))


def _ada_kernel(c_ref, w_ref, b_ref, o_ref):
    s = _silu(c_ref[...]).astype(BF16)
    acc = jnp.dot(s, w_ref[...].astype(BF16), preferred_element_type=F32)
    o_ref[...] = acc + b_ref[...]


def _ada_tables(cvec, ada_w, ada_b):
    tn = 1024
    n = N_MOD * D_MODEL
    blocks = _nbytes((D_MODEL, tn), F32) + _nbytes((MOD_ROWS, tn), F32) * 2
    return pl.pallas_call(
        _ada_kernel,
        grid=(DEPTH, n // tn),
        in_specs=[
            pl.BlockSpec((MOD_ROWS, D_MODEL), lambda l, j: (0, 0)),
            pl.BlockSpec((None, D_MODEL, tn), lambda l, j: (l, 0, j)),
            pl.BlockSpec((None, 1, tn), lambda l, j: (l, 0, j)),
        ],
        out_specs=pl.BlockSpec((None, MOD_ROWS, tn), lambda l, j: (l, 0, j)),
        out_shape=jax.ShapeDtypeStruct((DEPTH, MOD_ROWS, n), F32),
        compiler_params=pltpu.CompilerParams(
            dimension_semantics=("arbitrary", "arbitrary"),
            vmem_limit_bytes=_vmem_limit(blocks, _nbytes((D_MODEL, tn), BF16))),
        name="ada_tables",
    )(cvec, ada_w, ada_b.reshape(DEPTH, 1, n))


def _mod_specs(which, width, tm, col_of):
    tiles_per_batch = ROWS_B // tm

    def batch_map(i, *rest):
        return (i // tiles_per_batch, 0, col_of(which, i, *rest))

    def ctx_map(i, *rest):
        return (CTX_MOD_ROW, 0, col_of(which, i, *rest))

    return (pl.BlockSpec((None, 1, width), batch_map), pl.BlockSpec((None, 1, width), ctx_map))


def _normmod_kernel(x_ref, g_ref, shift_ref, scale_ref, o_ref):
    mul = g_ref[...] * (1.0 + scale_ref[...])
    add = shift_ref[...]

    def strip(r, carry):
        rows = pl.ds(pl.multiple_of(r * NORM_STRIP, NORM_STRIP), NORM_STRIP)
        x = x_ref[rows, :]
        inv = lax.rsqrt(jnp.mean(x * x, axis=-1, keepdims=True) + EPS)
        o_ref[rows, :] = (x * inv * mul + add).astype(o_ref.dtype)
        return carry

    lax.fori_loop(0, NORM_TM // NORM_STRIP, strip, 0)


def _normmod(xs, gain, modt, shift_idx, scale_idx):
    tiles_b = ROWS_B // NORM_TM

    def mod_spec(which):
        def index(i):
            is_ctx = (i % tiles_b) == tiles_b - 1
            return (jnp.where(is_ctx, CTX_MOD_ROW, i // tiles_b), 0, which)
        return pl.BlockSpec((None, 1, D_MODEL), index)

    blocks = _nbytes((NORM_TM, D_MODEL), F32) + _nbytes((NORM_TM, D_MODEL), BF16)
    return pl.pallas_call(
        _normmod_kernel,
        grid=(ROWS // NORM_TM,),
        in_specs=[pl.BlockSpec((NORM_TM, D_MODEL), lambda i: (i, 0)),
                  pl.BlockSpec((1, D_MODEL), lambda i: (0, 0)),
                  mod_spec(shift_idx), mod_spec(scale_idx)],
        out_specs=pl.BlockSpec((NORM_TM, D_MODEL), lambda i: (i, 0)),
        out_shape=jax.ShapeDtypeStruct((ROWS, D_MODEL), BF16),
        compiler_params=pltpu.CompilerParams(
            dimension_semantics=("arbitrary",),
            vmem_limit_bytes=_vmem_limit(blocks, 0)),
        name="normmod",
    )(xs, gain.reshape(1, D_MODEL), modt, modt)


def _rope_half_roll(acc, cos2, sin2):
    return acc * cos2 + pltpu.roll(acc, DA_HEAD_DIM // 2, 1) * sin2


def _proj_kernel(*refs, epi, tn, col_scale):
    if epi in ("rope128", "rope256"):
        a_ref, w_ref, cos_ref, sin_ref, o_ref = refs
    else:
        a_ref, w_ref, o_ref = refs
    acc = jnp.dot(a_ref[...], w_ref[...].astype(BF16), preferred_element_type=F32)
    if epi == "plain":
        o_ref[...] = acc.astype(o_ref.dtype)
    elif epi == "silu":
        o_ref[...] = _silu(acc).astype(o_ref.dtype)
    elif epi == "gelu":
        o_ref[...] = (0.5 * acc * (1.0 + lax.erf(acc * (2.0 ** -0.5)))).astype(o_ref.dtype)
    elif epi == "rope256":
        cos, sin = cos_ref[...], sin_ref[...]
        for g in range(tn // RET_KEY_DIM):
            lo = g * RET_KEY_DIM
            x1 = acc[:, lo:lo + LANE]
            x2 = acc[:, lo + LANE:lo + 2 * LANE]
            o_ref[:, lo:lo + LANE] = ((x1 * cos - x2 * sin) * col_scale).astype(o_ref.dtype)
            o_ref[:, lo + LANE:lo + 2 * LANE] = ((x1 * sin + x2 * cos) * col_scale).astype(o_ref.dtype)
    elif epi == "rope128":
        j = pl.program_id(1)
        tiles_per_part = DA_WIDTH // tn

        @pl.when(j < 2 * tiles_per_part)
        def _():
            cos2, sin2 = cos_ref[...], sin_ref[...]
            scale = jnp.where(j < tiles_per_part, col_scale, 1.0)
            for g in range(tn // DA_HEAD_DIM):
                lo = g * DA_HEAD_DIM
                r = _rope_half_roll(acc[:, lo:lo + DA_HEAD_DIM], cos2, sin2)
                o_ref[:, lo:lo + DA_HEAD_DIM] = (r * scale).astype(o_ref.dtype)

        @pl.when(j >= 2 * tiles_per_part)
        def _():
            o_ref[...] = acc.astype(o_ref.dtype)
    else:
        raise ValueError(epi)


def _proj(a, w_stack, layer, *, epi, out_dtype, tables=None, col_scale=1.0, name):
    _, k, n = w_stack.shape
    tm, tn = TM_WIDE, 512
    in_specs = [pl.BlockSpec((tm, k), lambda i, j: (i, 0)),
                pl.BlockSpec((None, k, tn), lambda i, j: (layer, 0, j))]
    args = [a, w_stack]
    if tables is not None:
        in_specs += [pl.BlockSpec((tm, LANE), lambda i, j: (i % (ROWS_B // tm), 0))] * 2
        args += list(tables)
    blocks = (_nbytes((tm, k), BF16) + _nbytes((k, tn), F32) + _nbytes((tm, tn), out_dtype)
              + 2 * _nbytes((tm, LANE), F32))
    temps = _nbytes((k, tn), BF16) + 2 * _nbytes((tm, tn), F32)
    return pl.pallas_call(
        functools.partial(_proj_kernel, epi=epi, tn=tn, col_scale=col_scale),
        grid=(ROWS // tm, n // tn),
        in_specs=in_specs,
        out_specs=pl.BlockSpec((tm, tn), lambda i, j: (i, j)),
        out_shape=jax.ShapeDtypeStruct((ROWS, n), out_dtype),
        compiler_params=pltpu.CompilerParams(
            dimension_semantics=("arbitrary", "arbitrary"),
            vmem_limit_bytes=_vmem_limit(blocks, temps)),
        name=name,
    )(*args)


def _glu_kernel(a_ref, wg_ref, wu_ref, o_ref):
    a = a_ref[...]
    gate = jnp.dot(a, wg_ref[...].astype(BF16), preferred_element_type=F32)
    up = jnp.dot(a, wu_ref[...].astype(BF16), preferred_element_type=F32)
    o_ref[...] = (_silu(gate) * up).astype(o_ref.dtype)


def _glu(a, w_stack, layer):
    k = a.shape[1]
    tm, tn = TM_WIDE, 512
    nt = FFN_HIDDEN // tn
    blocks = _nbytes((tm, k), BF16) + 2 * _nbytes((k, tn), F32) + _nbytes((tm, tn), BF16)
    temps = 2 * _nbytes((k, tn), BF16) + 3 * _nbytes((tm, tn), F32)
    return pl.pallas_call(
        _glu_kernel,
        grid=(ROWS // tm, nt),
        in_specs=[pl.BlockSpec((tm, k), lambda i, j: (i, 0)),
                  pl.BlockSpec((None, k, tn), lambda i, j: (layer, 0, j)),
                  pl.BlockSpec((None, k, tn), lambda i, j: (layer, 0, j + nt))],
        out_specs=pl.BlockSpec((tm, tn), lambda i, j: (i, j)),
        out_shape=jax.ShapeDtypeStruct((ROWS, FFN_HIDDEN), BF16),
        compiler_params=pltpu.CompilerParams(
            dimension_semantics=("arbitrary", "arbitrary"),
            vmem_limit_bytes=_vmem_limit(blocks, temps)),
        name="ffn_glu",
    )(a, w_stack, w_stack)


def _resid_kernel(a_ref, w_ref, x_ref, gb_ref, gc_ref, o_ref, *, tm):
    acc = jnp.dot(a_ref[...], w_ref[...].astype(BF16), preferred_element_type=F32)
    gate = jnp.where(_is_latent_rows(pl.program_id(0), tm), gb_ref[...], gc_ref[...])
    o_ref[...] = x_ref[...] + gate * acc


def _resid(a, w_stack, layer, xs, modt, gate_idx, *, name):
    k = w_stack.shape[1]
    if k <= D_MODEL:
        tm, tn = TM_WIDE, 512
    elif k <= 2 * D_MODEL:
        tm, tn = TM_DEEP, 512
    else:
        tm, tn = TM_DEEP, 256
    tiles_per_vec = D_MODEL // tn
    gb, gc = _mod_specs(gate_idx, tn, tm, lambda which, i, j: which * tiles_per_vec + j)
    blocks = _nbytes((tm, k), BF16) + _nbytes((k, tn), F32) + 2 * _nbytes((tm, tn), F32)
    temps = _nbytes((k, tn), BF16) + 2 * _nbytes((tm, tn), F32)
    return pl.pallas_call(
        functools.partial(_resid_kernel, tm=tm),
        grid=(ROWS // tm, D_MODEL // tn),
        in_specs=[pl.BlockSpec((tm, k), lambda i, j: (i, 0)),
                  pl.BlockSpec((None, k, tn), lambda i, j: (layer, 0, j)),
                  pl.BlockSpec((tm, tn), lambda i, j: (i, j)),
                  gb, gc],
        out_specs=pl.BlockSpec((tm, tn), lambda i, j: (i, j)),
        out_shape=jax.ShapeDtypeStruct((ROWS, D_MODEL), F32),
        compiler_params=pltpu.CompilerParams(
            dimension_semantics=("arbitrary", "arbitrary"),
            vmem_limit_bytes=_vmem_limit(blocks, temps)),
        name=name,
    )(a, w_stack, xs, modt, modt)


def _attn_scores(q_ref, k_ref, s_ref, m_ref, *, n_kv, rows_a, tq):
    for m in range(2):
        cols = slice(m * DA_HEAD_DIM, (m + 1) * DA_HEAD_DIM)
        qm = q_ref[:, cols]
        mx = None
        for c in range(n_kv // rows_a):
            rows = slice(c * rows_a, (c + 1) * rows_a)
            s = lax.dot_general(k_ref[rows, cols], qm, (((1,), (1,)), ((), ())), preferred_element_type=F32)
            s_ref[m, rows, :] = s
            part = jnp.max(s.reshape(rows_a // SUBLANE, SUBLANE, tq), axis=0)
            mx = part if mx is None else jnp.maximum(mx, part)
        m_ref[m] = jnp.broadcast_to(jnp.max(mx, axis=0, keepdims=True), (SUBLANE, tq))


def _attn_finish(lam, vt_ref, g_ref, s_ref, m_ref, o_ref, *, n_kv, tq, out_scale):
    heads = []
    for m in range(2):
        mrow = m_ref[m][0:1, :]
        acc = jnp.zeros((DA_HEAD_W, tq), F32)
        lsum = jnp.zeros((SUBLANE, tq), F32)
        for c in range(n_kv // MXU_DIM):
            rows = slice(c * MXU_DIM, (c + 1) * MXU_DIM)
            p = jnp.exp2(s_ref[m, rows, :] - mrow)
            lsum = lsum + jnp.sum(p.reshape(MXU_DIM // SUBLANE, SUBLANE, tq), axis=0)
            acc = acc + jnp.dot(vt_ref[:, rows], p.astype(BF16), preferred_element_type=F32)
        heads.append(acc * (1.0 / jnp.sum(lsum, axis=0, keepdims=True)))
    o = heads[0] - lam * heads[1]
    o = o * lax.rsqrt(jnp.mean(o * o, axis=0, keepdims=True) + EPS) * g_ref[...] * out_scale
    o_ref[...] = o.T.astype(o_ref.dtype)


def _diff_attn_kernel(lam_ref, q_ref, k_ref, v_ref, g_ref, o_ref, vt_ref, sa_ref, sb_ref, ma_ref, mb_ref,
                      *, n_kv, rows_a, tq, out_scale):
    t = pl.program_id(2)
    lam = lam_ref[0]
    scores = functools.partial(_attn_scores, q_ref, k_ref, n_kv=n_kv, rows_a=rows_a, tq=tq)
    finish = functools.partial(_attn_finish, lam, vt_ref, g_ref, n_kv=n_kv, tq=tq, out_scale=out_scale)

    @pl.when(t == 0)
    def _():
        for c in range(n_kv // MXU_DIM):
            rows = slice(c * MXU_DIM, (c + 1) * MXU_DIM)
            vt_ref[:, rows] = v_ref[rows, :].astype(F32).T.astype(BF16)
        scores(sa_ref, ma_ref)

    @pl.when(jnp.logical_and(t > 0, t % 2 == 0))
    def _():
        scores(sa_ref, ma_ref)
        finish(sb_ref, mb_ref, o_ref)

    @pl.when(t % 2 == 1)
    def _():
        scores(sb_ref, mb_ref)
        finish(sa_ref, ma_ref, o_ref)


def _diff_attn(qkv, lam, subln_g, out_scale, *, q_tiles, q_tile0, kv_rows, kv_blk0, prev=None):
    tq = CTX_LEN
    tiles_b = ROWS_B // tq
    kcol0 = DA_WIDTH // DA_HEAD_W
    vcol0 = 2 * kcol0
    rows_a = min(kv_rows, TM_DEEP)
    kv_blocks_b = ROWS_B // kv_rows
    blocks = 2 * _nbytes((tq, DA_HEAD_W), BF16) + 2 * _nbytes((kv_rows, DA_HEAD_W), BF16)
    scratch = (_nbytes((DA_HEAD_W, kv_rows), BF16) + 4 * _nbytes((kv_rows, tq), F32)
               + 4 * _nbytes((SUBLANE, tq), F32))
    in_specs = [
        pl.BlockSpec(memory_space=pltpu.SMEM),
        pl.BlockSpec((tq, DA_HEAD_W),
                     lambda b, h, t: (b * tiles_b + q_tile0 + jnp.minimum(t, q_tiles - 1), h)),
        pl.BlockSpec((kv_rows, DA_HEAD_W), lambda b, h, t: (b * kv_blocks_b + kv_blk0, kcol0 + h)),
        pl.BlockSpec((kv_rows, DA_HEAD_W), lambda b, h, t: (b * kv_blocks_b + kv_blk0, vcol0 + h)),
        pl.BlockSpec((DA_HEAD_W, 1), lambda b, h, t: (0, 0)),
    ]
    args = [lam, qkv, qkv, qkv, subln_g.reshape(DA_HEAD_W, 1)]
    aliases = {}
    if prev is not None:
        in_specs.append(pl.BlockSpec(memory_space=pl.ANY))
        args.append(prev)
        aliases = {5: 0}
    kern = functools.partial(_diff_attn_kernel, n_kv=kv_rows, rows_a=rows_a, tq=tq, out_scale=out_scale)

    def body(*refs):
        if prev is not None:
            refs = refs[:5] + refs[6:]
        kern(*refs)

    return pl.pallas_call(
        body,
        grid=(BATCH, DA_HEADS, q_tiles + 1),
        in_specs=in_specs,
        out_specs=pl.BlockSpec((tq, DA_HEAD_W),
                               lambda b, h, t: (b * tiles_b + q_tile0 + jnp.maximum(t - 1, 0), h)),
        out_shape=jax.ShapeDtypeStruct((ROWS, DA_WIDTH), BF16),
        scratch_shapes=[pltpu.VMEM((DA_HEAD_W, kv_rows), BF16),
                        pltpu.VMEM((2, kv_rows, tq), F32), pltpu.VMEM((2, kv_rows, tq), F32),
                        pltpu.VMEM((2, SUBLANE, tq), F32), pltpu.VMEM((2, SUBLANE, tq), F32)],
        input_output_aliases=aliases,
        compiler_params=pltpu.CompilerParams(
            dimension_semantics=("arbitrary", "arbitrary", "arbitrary"),
            vmem_limit_bytes=_vmem_limit(blocks, scratch + 4 * _nbytes((rows_a, tq), F32))),
        name="diff_attn",
    )(*args)


def _sg_gate_kernel(z_ref, vg_ref, ws_ref, bs_ref, o_ref, *, chunks):
    v = z_ref[:, SG_WIDTH:]
    v = v * lax.rsqrt(jnp.mean(v * v, axis=-1, keepdims=True) + EPS) * vg_ref[...]
    for c in range(chunks):
        r0 = c * SG_CHUNK
        for g in range(SG_GROUPS):
            c0 = g * SG_GROUP_DIM
            vg = v[r0:r0 + SG_CHUNK, c0:c0 + SG_GROUP_DIM].astype(BF16)
            mixed = jnp.dot(ws_ref[g].astype(BF16), vg, preferred_element_type=F32) + bs_ref[g]
            u = z_ref[r0:r0 + SG_CHUNK, c0:c0 + SG_GROUP_DIM]
            o_ref[r0:r0 + SG_CHUNK, c0:c0 + SG_GROUP_DIM] = (u * mixed).astype(o_ref.dtype)


def _sg_gate(z, v_gain, w_s, b_s):
    chunks = 2
    tm = chunks * SG_CHUNK
    bs_b = jnp.broadcast_to(b_s[:, :, None], (SG_GROUPS, SG_CHUNK, SG_GROUP_DIM))
    blocks = (_nbytes((tm, 2 * SG_WIDTH), F32) + _nbytes((tm, SG_WIDTH), BF16)
              + 2 * _nbytes((SG_GROUPS, SG_CHUNK, SG_CHUNK), F32))
    return pl.pallas_call(
        functools.partial(_sg_gate_kernel, chunks=chunks),
        grid=(ROWS // tm,),
        in_specs=[pl.BlockSpec((tm, 2 * SG_WIDTH), lambda i: (i, 0)),
                  pl.BlockSpec((1, SG_WIDTH), lambda i: (0, 0)),
                  pl.BlockSpec((SG_GROUPS, SG_CHUNK, SG_CHUNK), lambda i: (0, 0, 0)),
                  pl.BlockSpec((SG_GROUPS, SG_CHUNK, SG_GROUP_DIM), lambda i: (0, 0, 0))],
        out_specs=pl.BlockSpec((tm, SG_WIDTH), lambda i: (i, 0)),
        out_shape=jax.ShapeDtypeStruct((ROWS, SG_WIDTH), BF16),
        compiler_params=pltpu.CompilerParams(
            dimension_semantics=("arbitrary",),
            vmem_limit_bytes=_vmem_limit(blocks, 4 * _nbytes((tm, SG_WIDTH), F32))),
        name="sg_gate",
    )(z, v_gain.reshape(1, SG_WIDTH), w_s, bs_b)


def _ret_kernel(*refs, backward):
    if backward:
        g_ref, q_ref, k_ref, v_ref, dec_ref, xi_ref, zeta_ref, of_ref, gp_ref, o_ref, s_ref = refs
    else:
        g_ref, q_ref, k_ref, v_ref, dec_ref, xi_ref, zeta_ref, o_ref, s_ref = refs

    @pl.when(pl.program_id(1) == 0)
    def _():
        s_ref[...] = jnp.zeros_like(s_ref)

    for h in range(RET_HEADS):
        kc = slice(h * RET_KEY_DIM, (h + 1) * RET_KEY_DIM)
        vc = slice(h * RET_VAL_DIM, (h + 1) * RET_VAL_DIM)
        qh = q_ref[:, kc]
        kh = k_ref[:, kc]
        vh = v_ref[:, vc]
        scores = lax.dot_general(qh, kh, (((1,), (1,)), ((), ())), preferred_element_type=F32) * dec_ref[h]
        state = s_ref[h]
        o = jnp.dot(scores.astype(BF16), vh, preferred_element_type=F32)
        o = o + jnp.dot(qh, state.astype(BF16), preferred_element_type=F32) * xi_ref[h]
        kz_t = (kh.astype(F32) * zeta_ref[h]).T.astype(BF16)
        s_ref[h] = g_ref[h] * state + jnp.dot(kz_t, vh, preferred_element_type=F32)
        if backward:
            o = o + of_ref[:, vc]
            o = o * lax.rsqrt(jnp.mean(o * o, axis=-1, keepdims=True) + EPS)
            o_ref[:, vc] = (gp_ref[:, vc] * o).astype(o_ref.dtype)
        else:
            o_ref[:, vc] = o


def _ret_consts(log_gamma, backward):
    pos = jnp.arange(RET_CHUNK, dtype=F32)
    dist = pos[:, None] - pos[None, :]
    if backward:
        dist = -dist
    lg = log_gamma[:, None, None]
    decay = jnp.where(dist >= 0, jnp.exp(lg * jnp.maximum(dist, 0.0)), 0.0)
    pos_in_scan = (RET_CHUNK - 1.0 - pos) if backward else pos
    xi = jnp.exp(log_gamma[:, None] * (pos_in_scan + 1.0))
    zeta = jnp.exp(log_gamma[:, None] * (RET_CHUNK - 1.0 - pos_in_scan))
    g_chunk = jnp.exp(log_gamma * RET_CHUNK)
    bcast = lambda t, w: jnp.broadcast_to(t[:, :, None], (RET_HEADS, RET_CHUNK, w))
    return g_chunk, decay, bcast(xi, RET_VAL_DIM), bcast(zeta, RET_KEY_DIM)


def _ret_scan(q, k, v, log_gamma, *, backward, o_fwd=None, gproj=None):
    n_chunks = ROWS_B // RET_CHUNK
    x_chunks = SEQ // RET_CHUNK
    g_chunk, decay, xi, zeta = _ret_consts(log_gamma, backward)
    if backward:
        chunk_of = lambda b, t: (b * n_chunks + (n_chunks - 1 - t), 0)
    else:
        chunk_of = lambda b, t: (b * n_chunks + (t + x_chunks) % n_chunks, 0)
    qk_w = RET_HEADS * RET_KEY_DIM
    v_w = RET_HEADS * RET_VAL_DIM
    const_spec = lambda w: pl.BlockSpec((RET_HEADS, RET_CHUNK, w), lambda b, t: (0, 0, 0))
    in_specs = [pl.BlockSpec(memory_space=pltpu.SMEM),
                pl.BlockSpec((RET_CHUNK, qk_w), chunk_of),
                pl.BlockSpec((RET_CHUNK, qk_w), chunk_of),
                pl.BlockSpec((RET_CHUNK, v_w), chunk_of),
                const_spec(RET_CHUNK), const_spec(RET_VAL_DIM), const_spec(RET_KEY_DIM)]
    args = [g_chunk, q, k, v, decay, xi, zeta]
    blocks = (2 * _nbytes((RET_CHUNK, qk_w), BF16) + _nbytes((RET_CHUNK, v_w), BF16)
              + _nbytes((RET_HEADS, RET_CHUNK, RET_CHUNK + RET_VAL_DIM + RET_KEY_DIM), F32)
              + _nbytes((RET_CHUNK, v_w), F32))
    if backward:
        in_specs += [pl.BlockSpec((RET_CHUNK, v_w), chunk_of), pl.BlockSpec((RET_CHUNK, v_w), chunk_of)]
        args += [o_fwd, gproj]
        blocks += 2 * _nbytes((RET_CHUNK, v_w), F32)
        out_dtype = BF16
    else:
        out_dtype = F32
    state_bytes = _nbytes((RET_HEADS, RET_KEY_DIM, RET_VAL_DIM), F32)
    return pl.pallas_call(
        functools.partial(_ret_kernel, backward=backward),
        grid=(BATCH, n_chunks),
        in_specs=in_specs,
        out_specs=pl.BlockSpec((RET_CHUNK, v_w), chunk_of),
        out_shape=jax.ShapeDtypeStruct((ROWS, v_w), out_dtype),
        scratch_shapes=[pltpu.VMEM((RET_HEADS, RET_KEY_DIM, RET_VAL_DIM), F32)],
        compiler_params=pltpu.CompilerParams(
            dimension_semantics=("arbitrary", "arbitrary"),
            vmem_limit_bytes=_vmem_limit(blocks, state_bytes + (8 << 20))),
        name="ret_bwd" if backward else "ret_fwd",
    )(*args)


def _final_norm_kernel(x_ref, g_ref, o_ref):
    x = x_ref[...]
    o_ref[...] = x * lax.rsqrt(jnp.mean(x * x, axis=-1, keepdims=True) + EPS) * g_ref[...]


def _final_norm(xs, gain):
    blocks = 2 * _nbytes((NORM_TM, D_MODEL), F32)
    return pl.pallas_call(
        _final_norm_kernel,
        grid=(BATCH, SEQ // NORM_TM),
        in_specs=[pl.BlockSpec((NORM_TM, D_MODEL), lambda b, t: (b * (ROWS_B // NORM_TM) + t, 0)),
                  pl.BlockSpec((1, D_MODEL), lambda b, t: (0, 0))],
        out_specs=pl.BlockSpec((None, NORM_TM, D_MODEL), lambda b, t: (b, t, 0)),
        out_shape=jax.ShapeDtypeStruct((BATCH, SEQ, D_MODEL), F32),
        compiler_params=pltpu.CompilerParams(
            dimension_semantics=("arbitrary", "arbitrary"),
            vmem_limit_bytes=_vmem_limit(blocks, 2 * _nbytes((NORM_TM, D_MODEL), F32))),
        name="final_norm",
    )(xs, gain.reshape(1, D_MODEL))


def _rope_tables(head_dim):
    rows = SEQ // GRID_W
    row = jnp.broadcast_to(jnp.arange(rows)[:, None], (rows, GRID_W)).reshape(-1).astype(F32)
    col = jnp.broadcast_to(jnp.arange(GRID_W)[None, :], (rows, GRID_W)).reshape(-1).astype(F32)
    n_freq = head_dim // 4
    inv_freq = ROPE_BASE ** (-jnp.arange(n_freq, dtype=F32) / n_freq)
    ang = jnp.concatenate([row[:, None] * inv_freq, col[:, None] * inv_freq], axis=-1)
    cos = jnp.concatenate([jnp.cos(ang), jnp.ones((CTX_LEN, head_dim // 2), F32)], axis=0)
    sin = jnp.concatenate([jnp.sin(ang), jnp.zeros((CTX_LEN, head_dim // 2), F32)], axis=0)
    return cos, sin


def kernel(x, c, ctx, c_ctx, ada_w, ada_b, norm_mix_g, norm_ffn_g, ffn_w_gate_up, ffn_w_down,
           da_w_qkv, da_w_o, da_lambda, da_subln_g,
           sg_w_in, sg_v_g, sg_w_s, sg_b_s, sg_w_out,
           ret_w_q, ret_w_k, ret_w_v, ret_w_g, ret_w_o, ret_decay, final_norm_g):
    xs = jnp.concatenate([x, ctx], axis=1).reshape(ROWS, D_MODEL)
    cvec = jnp.concatenate([c, c_ctx[None, :], jnp.zeros((MOD_ROWS - BATCH - 1, D_MODEL), F32)], axis=0)
    mods = _ada_tables(cvec, ada_w, ada_b)

    da_cos, da_sin = _rope_tables(DA_HEAD_DIM)
    da_tables = (jnp.concatenate([da_cos, da_cos], axis=1), jnp.concatenate([-da_sin, da_sin], axis=1))
    ret_tables = _rope_tables(RET_KEY_DIM)
    da_q_scale = DA_HEAD_DIM ** -0.5 * math.log2(math.e)

    for i in range(DEPTH):
        kind = i % N_MIXERS
        j = i // N_MIXERS
        modt = mods[i].reshape(MOD_ROWS, 1, N_MOD * D_MODEL)
        h = _normmod(xs, norm_mix_g[i], modt, 0, 1)
        if kind == 0:
            lambda_init = 0.8 - 0.6 * math.exp(-0.3 * i)
            lv = da_lambda[j].astype(F32)
            lam = (jnp.exp(jnp.sum(lv[0] * lv[1])) - jnp.exp(jnp.sum(lv[2] * lv[3])) + lambda_init).reshape(1)
            qkv = _proj(h, da_w_qkv, j, epi="rope128", out_dtype=BF16, tables=da_tables,
                        col_scale=da_q_scale, name="da_qkv")
            attn = functools.partial(_diff_attn, qkv, lam, da_subln_g[j], 1.0 - lambda_init)
            y = attn(q_tiles=SEQ // CTX_LEN, q_tile0=0, kv_rows=ROWS_B, kv_blk0=0)
            y = attn(q_tiles=1, q_tile0=SEQ // CTX_LEN, kv_rows=CTX_LEN, kv_blk0=SEQ // CTX_LEN, prev=y)
            xs = _resid(y, da_w_o, j, xs, modt, 2, name="da_out")
        elif kind == 1:
            z = _proj(h, sg_w_in, j, epi="gelu", out_dtype=F32, name="sg_in")
            y = _sg_gate(z, sg_v_g[j], sg_w_s[j], sg_b_s[j])
            xs = _resid(y, sg_w_out, j, xs, modt, 2, name="sg_out")
        else:
            log_gamma = -jnp.exp(ret_decay[j].astype(F32))
            q = _proj(h, ret_w_q, j, epi="rope256", out_dtype=BF16, tables=ret_tables, name="ret_q")
            k = _proj(h, ret_w_k, j, epi="rope256", out_dtype=BF16, tables=ret_tables,
                      col_scale=RET_KEY_DIM ** -0.5, name="ret_k")
            v = _proj(h, ret_w_v, j, epi="plain", out_dtype=BF16, name="ret_v")
            gp = _proj(h, ret_w_g, j, epi="silu", out_dtype=F32, name="ret_g")
            o_f = _ret_scan(q, k, v, log_gamma[0], backward=False)
            y = _ret_scan(q, k, v, log_gamma[1], backward=True, o_fwd=o_f, gproj=gp)
            xs = _resid(y, ret_w_o, j, xs, modt, 2, name="ret_out")
        h = _normmod(xs, norm_ffn_g[i], modt, 3, 4)
        act = _glu(h, ffn_w_gate_up, i)
        xs = _resid(act, ffn_w_down, i, xs, modt, 5, name="ffn_down")
    return _final_norm(xs, final_norm_g)
```

```python
import functools
import math

import jax
import jax.numpy as jnp
from jax import lax
from jax.experimental import pallas as pl
from jax.experimental.pallas import tpu as pltpu

D_MODEL = 2048
BATCH = 2
SEQ = 4096
DEPTH = 4
GRID_W = 64
CTX_LEN = 256
N_MIXERS = 3
EPS = 1e-6
ROPE_BASE = 10000.0
N_MOD = 6

DA_HEAD_DIM = 128
DA_HEADS = D_MODEL // (2 * DA_HEAD_DIM)
DA_WIDTH = 2 * DA_HEADS * DA_HEAD_DIM
DA_HEAD_W = 2 * DA_HEAD_DIM

SG_CHUNK = 128
SG_GROUP_DIM = 128
SG_WIDTH = D_MODEL
SG_GROUPS = SG_WIDTH // SG_GROUP_DIM

RET_HEADS = D_MODEL // 256
RET_KEY_DIM = D_MODEL // RET_HEADS
RET_VAL_DIM = 2 * D_MODEL // RET_HEADS
RET_CHUNK = 256

FFN_HIDDEN = -((-8 * D_MODEL) // (3 * 256)) * 256

ROWS_B = SEQ + CTX_LEN
ROWS = BATCH * ROWS_B
TM_WIDE = ROWS_B // 2
TM_DEEP = ROWS_B // 4
NORM_TM = CTX_LEN
NORM_STRIP = 16
MOD_ROWS = 8
CTX_MOD_ROW = BATCH

LANE = 128
SUBLANE = 8
MXU_DIM = 256
VMEM_LIMIT_CAP = 60 * 1024 * 1024

BF16 = jnp.bfloat16
F32 = jnp.float32


def _vmem_limit(block_bytes, temp_bytes):
    return int(min(VMEM_LIMIT_CAP, 2 * block_bytes + temp_bytes + (4 << 20)))


def _nbytes(shape, dtype):
    return math.prod(shape) * jnp.dtype(dtype).itemsize


def _is_latent_rows(tile_idx, tm):
    row = lax.broadcasted_iota(jnp.int32, (tm, 1), 0) + (tile_idx % (ROWS_B // tm)) * tm
    return row < SEQ


def _silu(x):
    return x * (1.0 / (1.0 + jnp.exp(-x)))


def _ada_kernel(c_ref, w_ref, b_ref, o_ref):
    s = _silu(c_ref[...]).astype(BF16)
    acc = jnp.dot(s, w_ref[...].astype(BF16), preferred_element_type=F32)
    o_ref[...] = acc + b_ref[...]


def _ada_tables(cvec, ada_w, ada_b):
    tn = 1024
    n = N_MOD * D_MODEL
    blocks = _nbytes((D_MODEL, tn), F32) + _nbytes((MOD_ROWS, tn), F32) * 2
    return pl.pallas_call(
        _ada_kernel,
        grid=(DEPTH, n // tn),
        in_specs=[
            pl.BlockSpec((MOD_ROWS, D_MODEL), lambda l, j: (0, 0)),
            pl.BlockSpec((None, D_MODEL, tn), lambda l, j: (l, 0, j)),
            pl.BlockSpec((None, 1, tn), lambda l, j: (l, 0, j)),
        ],
        out_specs=pl.BlockSpec((None, MOD_ROWS, tn), lambda l, j: (l, 0, j)),
        out_shape=jax.ShapeDtypeStruct((DEPTH, MOD_ROWS, n), F32),
        compiler_params=pltpu.CompilerParams(
            dimension_semantics=("arbitrary", "arbitrary"),
            vmem_limit_bytes=_vmem_limit(blocks, _nbytes((D_MODEL, tn), BF16))),
        name="ada_tables",
    )(cvec, ada_w, ada_b.reshape(DEPTH, 1, n))


def _mod_spec(layer, which, width, col_of=lambda *idx: 0):
    blocks_per_vec = D_MODEL // width
    return pl.BlockSpec((None, MOD_ROWS, width), lambda *idx: (layer, 0, which * blocks_per_vec + col_of(*idx)))


def _normmod_kernel(x_ref, g_ref, shift_ref, scale_ref, o_ref):
    i = pl.program_id(0)
    tiles_b = ROWS_B // NORM_TM
    row = pl.ds(jnp.where(i % tiles_b == tiles_b - 1, CTX_MOD_ROW, i // tiles_b), 1)
    mul = g_ref[...] * (1.0 + scale_ref[row, :])
    add = shift_ref[row, :]
    for r in range(NORM_TM // NORM_STRIP):
        rows = slice(r * NORM_STRIP, (r + 1) * NORM_STRIP)
        x = x_ref[rows, :]
        inv = lax.rsqrt(jnp.mean(x * x, axis=-1, keepdims=True) + EPS)
        o_ref[rows, :] = (x * inv * mul + add).astype(o_ref.dtype)


def _normmod(xs, gain, mods, layer, shift_idx, scale_idx):
    blocks = _nbytes((NORM_TM, D_MODEL), F32) + _nbytes((NORM_TM, D_MODEL), BF16)
    return pl.pallas_call(
        _normmod_kernel,
        grid=(ROWS // NORM_TM,),
        in_specs=[pl.BlockSpec((NORM_TM, D_MODEL), lambda i: (i, 0)),
                  pl.BlockSpec((1, D_MODEL), lambda i: (0, 0)),
                  _mod_spec(layer, shift_idx, D_MODEL), _mod_spec(layer, scale_idx, D_MODEL)],
        out_specs=pl.BlockSpec((NORM_TM, D_MODEL), lambda i: (i, 0)),
        out_shape=jax.ShapeDtypeStruct((ROWS, D_MODEL), BF16),
        compiler_params=pltpu.CompilerParams(
            dimension_semantics=("arbitrary",),
            vmem_limit_bytes=_vmem_limit(blocks, 0)),
        name="normmod",
    )(xs, gain.reshape(1, D_MODEL), mods, mods)


def _rope_half_roll(acc, cos2, sin2):
    return acc * cos2 + pltpu.roll(acc, DA_HEAD_DIM // 2, 1) * sin2


def _proj_kernel(*refs, epi, tn, col_scale):
    if epi in ("rope128", "rope256"):
        a_ref, w_ref, cos_ref, sin_ref, o_ref = refs
    else:
        a_ref, w_ref, o_ref = refs
    acc = jnp.dot(a_ref[...], w_ref[...].astype(BF16), preferred_element_type=F32)
    if epi == "plain":
        o_ref[...] = acc.astype(o_ref.dtype)
    elif epi == "silu":
        o_ref[...] = _silu(acc).astype(o_ref.dtype)
    elif epi == "gelu":
        o_ref[...] = (0.5 * acc * (1.0 + lax.erf(acc * (2.0 ** -0.5)))).astype(o_ref.dtype)
    elif epi == "rope256":
        cos, sin = cos_ref[...], sin_ref[...]
        for g in range(tn // RET_KEY_DIM):
            lo = g * RET_KEY_DIM
            x1 = acc[:, lo:lo + LANE]
            x2 = acc[:, lo + LANE:lo + 2 * LANE]
            o_ref[:, lo:lo + LANE] = ((x1 * cos - x2 * sin) * col_scale).astype(o_ref.dtype)
            o_ref[:, lo + LANE:lo + 2 * LANE] = ((x1 * sin + x2 * cos) * col_scale).astype(o_ref.dtype)
    elif epi == "rope128":
        j = pl.program_id(1)
        tiles_per_part = DA_WIDTH // tn

        @pl.when(j < 2 * tiles_per_part)
        def _():
            cos2, sin2 = cos_ref[...], sin_ref[...]
            scale = jnp.where(j < tiles_per_part, col_scale, 1.0)
            for g in range(tn // DA_HEAD_DIM):
                lo = g * DA_HEAD_DIM
                r = _rope_half_roll(acc[:, lo:lo + DA_HEAD_DIM], cos2, sin2)
                o_ref[:, lo:lo + DA_HEAD_DIM] = (r * scale).astype(o_ref.dtype)

        @pl.when(j >= 2 * tiles_per_part)
        def _():
            o_ref[...] = acc.astype(o_ref.dtype)
    else:
        raise ValueError(epi)


def _proj(a, w_stack, layer, *, epi, out_dtype, tables=None, col_scale=1.0, name):
    _, k, n = w_stack.shape
    tm, tn = TM_WIDE, 512
    in_specs = [pl.BlockSpec((tm, k), lambda i, j: (i, 0)),
                pl.BlockSpec((None, k, tn), lambda i, j: (layer, 0, j))]
    args = [a, w_stack]
    if tables is not None:
        in_specs += [pl.BlockSpec((tm, LANE), lambda i, j: (i % (ROWS_B // tm), 0))] * 2
        args += list(tables)
    blocks = (_nbytes((tm, k), BF16) + _nbytes((k, tn), F32) + _nbytes((tm, tn), out_dtype)
              + 2 * _nbytes((tm, LANE), F32))
    temps = _nbytes((k, tn), BF16) + 2 * _nbytes((tm, tn), F32)
    return pl.pallas_call(
        functools.partial(_proj_kernel, epi=epi, tn=tn, col_scale=col_scale),
        grid=(ROWS // tm, n // tn),
        in_specs=in_specs,
        out_specs=pl.BlockSpec((tm, tn), lambda i, j: (i, j)),
        out_shape=jax.ShapeDtypeStruct((ROWS, n), out_dtype),
        compiler_params=pltpu.CompilerParams(
            dimension_semantics=("arbitrary", "arbitrary"),
            vmem_limit_bytes=_vmem_limit(blocks, temps)),
        name=name,
    )(*args)


def _glu_kernel(a_ref, wg_ref, wu_ref, o_ref, *, tn):
    a = a_ref[...]
    for c in range(tn // MXU_DIM):
        cols = slice(c * MXU_DIM, (c + 1) * MXU_DIM)
        gate = jnp.dot(a, wg_ref[:, cols].astype(BF16), preferred_element_type=F32)
        up = jnp.dot(a, wu_ref[:, cols].astype(BF16), preferred_element_type=F32)
        o_ref[:, cols] = (_silu(gate) * up).astype(o_ref.dtype)


def _glu(a, w_stack, layer):
    k = a.shape[1]
    tm, tn = TM_WIDE, 512
    nt = FFN_HIDDEN // tn
    blocks = _nbytes((tm, k), BF16) + 2 * _nbytes((k, tn), F32) + _nbytes((tm, tn), BF16)
    temps = 2 * _nbytes((k, tn), BF16) + 3 * _nbytes((tm, tn), F32)
    return pl.pallas_call(
        functools.partial(_glu_kernel, tn=tn),
        grid=(ROWS // tm, nt),
        in_specs=[pl.BlockSpec((tm, k), lambda i, j: (i, 0)),
                  pl.BlockSpec((None, k, tn), lambda i, j: (layer, 0, j)),
                  pl.BlockSpec((None, k, tn), lambda i, j: (layer, 0, j + nt))],
        out_specs=pl.BlockSpec((tm, tn), lambda i, j: (i, j)),
        out_shape=jax.ShapeDtypeStruct((ROWS, FFN_HIDDEN), BF16),
        compiler_params=pltpu.CompilerParams(
            dimension_semantics=("arbitrary", "arbitrary"),
            vmem_limit_bytes=_vmem_limit(blocks, temps)),
        name="ffn_glu",
    )(a, w_stack, w_stack)


def _resid_kernel(a_ref, w_ref, x_ref, g_ref, o_ref, *, tm):
    i = pl.program_id(0)
    acc = jnp.dot(a_ref[...], w_ref[...].astype(BF16), preferred_element_type=F32)
    gate_b = g_ref[pl.ds(i // (ROWS_B // tm), 1), :]
    gate_c = g_ref[CTX_MOD_ROW:CTX_MOD_ROW + 1, :]
    gate = jnp.where(_is_latent_rows(i, tm), gate_b, gate_c)
    o_ref[...] = x_ref[...] + gate * acc


def _resid(a, w_stack, layer, xs, mods, mod_layer, gate_idx, *, name):
    k = w_stack.shape[1]
    if k <= D_MODEL:
        tm, tn = TM_WIDE, 512
    elif k <= 2 * D_MODEL:
        tm, tn = TM_DEEP, 512
    else:
        tm, tn = TM_DEEP, 256
    blocks = _nbytes((tm, k), BF16) + _nbytes((k, tn), F32) + 2 * _nbytes((tm, tn), F32)
    temps = _nbytes((k, tn), BF16) + 2 * _nbytes((tm, tn), F32)
    return pl.pallas_call(
        functools.partial(_resid_kernel, tm=tm),
        grid=(ROWS // tm, D_MODEL // tn),
        in_specs=[pl.BlockSpec((tm, k), lambda i, j: (i, 0)),
                  pl.BlockSpec((None, k, tn), lambda i, j: (layer, 0, j)),
                  pl.BlockSpec((tm, tn), lambda i, j: (i, j)),
                  _mod_spec(mod_layer, gate_idx, tn, lambda i, j: j)],
        out_specs=pl.BlockSpec((tm, tn), lambda i, j: (i, j)),
        out_shape=jax.ShapeDtypeStruct((ROWS, D_MODEL), F32),
        compiler_params=pltpu.CompilerParams(
            dimension_semantics=("arbitrary", "arbitrary"),
            vmem_limit_bytes=_vmem_limit(blocks, temps)),
        name=name,
    )(a, w_stack, xs, mods)


def _attn_scores(q_ref, k_ref, s_ref, m_ref, *, n_kv, rows_a, tq):
    for m in range(2):
        cols = slice(m * DA_HEAD_DIM, (m + 1) * DA_HEAD_DIM)
        qm = q_ref[:, cols]
        mx = None
        for c in range(n_kv // rows_a):
            rows = slice(c * rows_a, (c + 1) * rows_a)
            s = lax.dot_general(k_ref[rows, cols], qm, (((1,), (1,)), ((), ())), preferred_element_type=F32)
            s_ref[m, rows, :] = s
            part = jnp.max(s.reshape(rows_a // SUBLANE, SUBLANE, tq), axis=0)
            mx = part if mx is None else jnp.maximum(mx, part)
        m_ref[m] = jnp.broadcast_to(jnp.max(mx, axis=0, keepdims=True), (SUBLANE, tq))


def _attn_finish(lam, vt_ref, g_ref, s_ref, m_ref, o_ref, *, n_kv, tq, out_scale):
    heads = []
    for m in range(2):
        mrow = m_ref[m][0:1, :]
        acc = jnp.zeros((DA_HEAD_W, tq), F32)
        lsum = jnp.zeros((SUBLANE, tq), F32)
        for c in range(n_kv // MXU_DIM):
            rows = slice(c * MXU_DIM, (c + 1) * MXU_DIM)
            p = jnp.exp2(s_ref[m, rows, :] - mrow)
            lsum = lsum + jnp.sum(p.reshape(MXU_DIM // SUBLANE, SUBLANE, tq), axis=0)
            acc = acc + jnp.dot(vt_ref[:, rows], p.astype(BF16), preferred_element_type=F32)
        heads.append(acc * (1.0 / jnp.sum(lsum, axis=0, keepdims=True)))
    o = heads[0] - lam * heads[1]
    o = o * lax.rsqrt(jnp.mean(o * o, axis=0, keepdims=True) + EPS) * g_ref[...] * out_scale
    o_ref[...] = o.T.astype(o_ref.dtype)


def _diff_attn_kernel(lam_ref, q_ref, k_ref, v_ref, g_ref, o_ref, vt_ref, sa_ref, sb_ref, ma_ref, mb_ref,
                      *, n_kv, rows_a, tq, out_scale):
    t = pl.program_id(2)
    lam = lam_ref[0]
    scores = functools.partial(_attn_scores, q_ref, k_ref, n_kv=n_kv, rows_a=rows_a, tq=tq)
    finish = functools.partial(_attn_finish, lam, vt_ref, g_ref, n_kv=n_kv, tq=tq, out_scale=out_scale)

    @pl.when(t == 0)
    def _():
        for c in range(n_kv // MXU_DIM):
            rows = slice(c * MXU_DIM, (c + 1) * MXU_DIM)
            vt_ref[:, rows] = v_ref[rows, :].astype(F32).T.astype(BF16)
        scores(sa_ref, ma_ref)

    @pl.when(jnp.logical_and(t > 0, t % 2 == 0))
    def _():
        scores(sa_ref, ma_ref)
        finish(sb_ref, mb_ref, o_ref)

    @pl.when(t % 2 == 1)
    def _():
        scores(sb_ref, mb_ref)
        finish(sa_ref, ma_ref, o_ref)


def _diff_attn(qkv, lam, subln_g, out_scale, *, q_tiles, q_tile0, kv_rows, kv_blk0, prev=None):
    tq = CTX_LEN
    tiles_b = ROWS_B // tq
    kcol0 = DA_WIDTH // DA_HEAD_W
    vcol0 = 2 * kcol0
    rows_a = min(kv_rows, TM_DEEP)
    kv_blocks_b = ROWS_B // kv_rows
    blocks = 2 * _nbytes((tq, DA_HEAD_W), BF16) + 2 * _nbytes((kv_rows, DA_HEAD_W), BF16)
    scratch = (_nbytes((DA_HEAD_W, kv_rows), BF16) + 4 * _nbytes((kv_rows, tq), F32)
               + 4 * _nbytes((SUBLANE, tq), F32))
    in_specs = [
        pl.BlockSpec(memory_space=pltpu.SMEM),
        pl.BlockSpec((tq, DA_HEAD_W),
                     lambda b, h, t: (b * tiles_b + q_tile0 + jnp.minimum(t, q_tiles - 1), h)),
        pl.BlockSpec((kv_rows, DA_HEAD_W), lambda b, h, t: (b * kv_blocks_b + kv_blk0, kcol0 + h)),
        pl.BlockSpec((kv_rows, DA_HEAD_W), lambda b, h, t: (b * kv_blocks_b + kv_blk0, vcol0 + h)),
        pl.BlockSpec((DA_HEAD_W, 1), lambda b, h, t: (0, 0)),
    ]
    args = [lam, qkv, qkv, qkv, subln_g.reshape(DA_HEAD_W, 1)]
    aliases = {}
    if prev is not None:
        in_specs.append(pl.BlockSpec(memory_space=pl.ANY))
        args.append(prev)
        aliases = {5: 0}
    kern = functools.partial(_diff_attn_kernel, n_kv=kv_rows, rows_a=rows_a, tq=tq, out_scale=out_scale)

    def body(*refs):
        if prev is not None:
            refs = refs[:5] + refs[6:]
        kern(*refs)

    return pl.pallas_call(
        body,
        grid=(BATCH, DA_HEADS, q_tiles + 1),
        in_specs=in_specs,
        out_specs=pl.BlockSpec((tq, DA_HEAD_W),
                               lambda b, h, t: (b * tiles_b + q_tile0 + jnp.maximum(t - 1, 0), h)),
        out_shape=jax.ShapeDtypeStruct((ROWS, DA_WIDTH), BF16),
        scratch_shapes=[pltpu.VMEM((DA_HEAD_W, kv_rows), BF16),
                        pltpu.VMEM((2, kv_rows, tq), F32), pltpu.VMEM((2, kv_rows, tq), F32),
                        pltpu.VMEM((2, SUBLANE, tq), F32), pltpu.VMEM((2, SUBLANE, tq), F32)],
        input_output_aliases=aliases,
        compiler_params=pltpu.CompilerParams(
            dimension_semantics=("arbitrary", "arbitrary", "arbitrary"),
            vmem_limit_bytes=_vmem_limit(blocks, scratch + 4 * _nbytes((rows_a, tq), F32))),
        name="diff_attn",
    )(*args)


def _sg_gate_kernel(z_ref, vg_ref, ws_ref, bs_ref, o_ref, *, chunks):
    v = z_ref[:, SG_WIDTH:]
    v = v * lax.rsqrt(jnp.mean(v * v, axis=-1, keepdims=True) + EPS) * vg_ref[...]
    for c in range(chunks):
        r0 = c * SG_CHUNK
        for g in range(SG_GROUPS):
            c0 = g * SG_GROUP_DIM
            vg = v[r0:r0 + SG_CHUNK, c0:c0 + SG_GROUP_DIM].astype(BF16)
            mixed = jnp.dot(ws_ref[g].astype(BF16), vg, preferred_element_type=F32) + bs_ref[g]
            u = z_ref[r0:r0 + SG_CHUNK, c0:c0 + SG_GROUP_DIM]
            o_ref[r0:r0 + SG_CHUNK, c0:c0 + SG_GROUP_DIM] = (u * mixed).astype(o_ref.dtype)


def _sg_gate(z, v_gain, w_s, b_s):
    chunks = 2
    tm = chunks * SG_CHUNK
    bs_b = jnp.broadcast_to(b_s[:, :, None], (SG_GROUPS, SG_CHUNK, SG_GROUP_DIM))
    blocks = (_nbytes((tm, 2 * SG_WIDTH), F32) + _nbytes((tm, SG_WIDTH), BF16)
              + 2 * _nbytes((SG_GROUPS, SG_CHUNK, SG_CHUNK), F32))
    return pl.pallas_call(
        functools.partial(_sg_gate_kernel, chunks=chunks),
        grid=(ROWS // tm,),
        in_specs=[pl.BlockSpec((tm, 2 * SG_WIDTH), lambda i: (i, 0)),
                  pl.BlockSpec((1, SG_WIDTH), lambda i: (0, 0)),
                  pl.BlockSpec((SG_GROUPS, SG_CHUNK, SG_CHUNK), lambda i: (0, 0, 0)),
                  pl.BlockSpec((SG_GROUPS, SG_CHUNK, SG_GROUP_DIM), lambda i: (0, 0, 0))],
        out_specs=pl.BlockSpec((tm, SG_WIDTH), lambda i: (i, 0)),
        out_shape=jax.ShapeDtypeStruct((ROWS, SG_WIDTH), BF16),
        compiler_params=pltpu.CompilerParams(
            dimension_semantics=("arbitrary",),
            vmem_limit_bytes=_vmem_limit(blocks, 4 * _nbytes((tm, SG_WIDTH), F32))),
        name="sg_gate",
    )(z, v_gain.reshape(1, SG_WIDTH), w_s, bs_b)


def _ret_kernel(*refs, backward):
    if backward:
        g_ref, q_ref, k_ref, v_ref, dec_ref, xi_ref, zeta_ref, of_ref, gp_ref, o_ref, s_ref = refs
    else:
        g_ref, q_ref, k_ref, v_ref, dec_ref, xi_ref, zeta_ref, o_ref, s_ref = refs

    @pl.when(pl.program_id(1) == 0)
    def _():
        s_ref[...] = jnp.zeros_like(s_ref)

    for h in range(RET_HEADS):
        kc = slice(h * RET_KEY_DIM, (h + 1) * RET_KEY_DIM)
        vc = slice(h * RET_VAL_DIM, (h + 1) * RET_VAL_DIM)
        qh = q_ref[:, kc]
        kh = k_ref[:, kc]
        vh = v_ref[:, vc]
        scores = lax.dot_general(qh, kh, (((1,), (1,)), ((), ())), preferred_element_type=F32) * dec_ref[h]
        state = s_ref[h]
        o = jnp.dot(scores.astype(BF16), vh, preferred_element_type=F32)
        cross = jnp.dot(qh, state.astype(BF16), preferred_element_type=F32)
        xi = xi_ref[h]
        o = o + jnp.concatenate(
            [cross[:, e * LANE:(e + 1) * LANE] * xi for e in range(RET_VAL_DIM // LANE)], axis=1)
        zeta = zeta_ref[h]
        kz = jnp.concatenate(
            [kh[:, e * LANE:(e + 1) * LANE].astype(F32) * zeta for e in range(RET_KEY_DIM // LANE)], axis=1)
        kz_t = kz.T.astype(BF16)
        s_ref[h] = g_ref[h] * state + jnp.dot(kz_t, vh, preferred_element_type=F32)
        if backward:
            o = o + of_ref[:, vc]
            o = o * lax.rsqrt(jnp.mean(o * o, axis=-1, keepdims=True) + EPS)
            o_ref[:, vc] = (gp_ref[:, vc] * o).astype(o_ref.dtype)
        else:
            o_ref[:, vc] = o


def _ret_consts(log_gamma, backward):
    pos = jnp.arange(RET_CHUNK, dtype=F32)
    dist = pos[:, None] - pos[None, :]
    if backward:
        dist = -dist
    lg = log_gamma[:, None, None]
    decay = jnp.where(dist >= 0, jnp.exp(lg * jnp.maximum(dist, 0.0)), 0.0)
    pos_in_scan = (RET_CHUNK - 1.0 - pos) if backward else pos
    xi = jnp.exp(log_gamma[:, None] * (pos_in_scan + 1.0))
    zeta = jnp.exp(log_gamma[:, None] * (RET_CHUNK - 1.0 - pos_in_scan))
    g_chunk = jnp.exp(log_gamma * RET_CHUNK)
    bcast = lambda t: jnp.broadcast_to(t[:, :, None], (RET_HEADS, RET_CHUNK, LANE))
    return g_chunk, decay, bcast(xi), bcast(zeta)


def _ret_scan(q, k, v, log_gamma, *, backward, o_fwd=None, gproj=None):
    n_chunks = ROWS_B // RET_CHUNK
    x_chunks = SEQ // RET_CHUNK
    g_chunk, decay, xi, zeta = _ret_consts(log_gamma, backward)
    if backward:
        chunk_of = lambda b, t: (b * n_chunks + (n_chunks - 1 - t), 0)
    else:
        chunk_of = lambda b, t: (b * n_chunks + (t + x_chunks) % n_chunks, 0)
    qk_w = RET_HEADS * RET_KEY_DIM
    v_w = RET_HEADS * RET_VAL_DIM
    const_spec = lambda w: pl.BlockSpec((RET_HEADS, RET_CHUNK, w), lambda b, t: (0, 0, 0))
    in_specs = [pl.BlockSpec(memory_space=pltpu.SMEM),
                pl.BlockSpec((RET_CHUNK, qk_w), chunk_of),
                pl.BlockSpec((RET_CHUNK, qk_w), chunk_of),
                pl.BlockSpec((RET_CHUNK, v_w), chunk_of),
                const_spec(RET_CHUNK), const_spec(LANE), const_spec(LANE)]
    args = [g_chunk, q, k, v, decay, xi, zeta]
    blocks = (2 * _nbytes((RET_CHUNK, qk_w), BF16) + _nbytes((RET_CHUNK, v_w), BF16)
              + _nbytes((RET_HEADS, RET_CHUNK, RET_CHUNK + 2 * LANE), F32)
              + _nbytes((RET_CHUNK, v_w), F32))
    if backward:
        in_specs += [pl.BlockSpec((RET_CHUNK, v_w), chunk_of), pl.BlockSpec((RET_CHUNK, v_w), chunk_of)]
        args += [o_fwd, gproj]
        blocks += 2 * _nbytes((RET_CHUNK, v_w), F32)
        out_dtype = BF16
    else:
        out_dtype = F32
    state_bytes = _nbytes((RET_HEADS, RET_KEY_DIM, RET_VAL_DIM), F32)
    return pl.pallas_call(
        functools.partial(_ret_kernel, backward=backward),
        grid=(BATCH, n_chunks),
        in_specs=in_specs,
        out_specs=pl.BlockSpec((RET_CHUNK, v_w), chunk_of),
        out_shape=jax.ShapeDtypeStruct((ROWS, v_w), out_dtype),
        scratch_shapes=[pltpu.VMEM((RET_HEADS, RET_KEY_DIM, RET_VAL_DIM), F32)],
        compiler_params=pltpu.CompilerParams(
            dimension_semantics=("arbitrary", "arbitrary"),
            vmem_limit_bytes=_vmem_limit(blocks, state_bytes + (8 << 20))),
        name="ret_bwd" if backward else "ret_fwd",
    )(*args)


def _final_norm_kernel(x_ref, g_ref, o_ref):
    x = x_ref[...]
    o_ref[...] = x * lax.rsqrt(jnp.mean(x * x, axis=-1, keepdims=True) + EPS) * g_ref[...]


def _final_norm(xs, gain):
    blocks = 2 * _nbytes((NORM_TM, D_MODEL), F32)
    return pl.pallas_call(
        _final_norm_kernel,
        grid=(BATCH, SEQ // NORM_TM),
        in_specs=[pl.BlockSpec((NORM_TM, D_MODEL), lambda b, t: (b * (ROWS_B // NORM_TM) + t, 0)),
                  pl.BlockSpec((1, D_MODEL), lambda b, t: (0, 0))],
        out_specs=pl.BlockSpec((None, NORM_TM, D_MODEL), lambda b, t: (b, t, 0)),
        out_shape=jax.ShapeDtypeStruct((BATCH, SEQ, D_MODEL), F32),
        compiler_params=pltpu.CompilerParams(
            dimension_semantics=("arbitrary", "arbitrary"),
            vmem_limit_bytes=_vmem_limit(blocks, 2 * _nbytes((NORM_TM, D_MODEL), F32))),
        name="final_norm",
    )(xs, gain.reshape(1, D_MODEL))


def _rope_tables(head_dim):
    rows = SEQ // GRID_W
    row = jnp.broadcast_to(jnp.arange(rows)[:, None], (rows, GRID_W)).reshape(-1).astype(F32)
    col = jnp.broadcast_to(jnp.arange(GRID_W)[None, :], (rows, GRID_W)).reshape(-1).astype(F32)
    n_freq = head_dim // 4
    inv_freq = ROPE_BASE ** (-jnp.arange(n_freq, dtype=F32) / n_freq)
    ang = jnp.concatenate([row[:, None] * inv_freq, col[:, None] * inv_freq], axis=-1)
    cos = jnp.concatenate([jnp.cos(ang), jnp.ones((CTX_LEN, head_dim // 2), F32)], axis=0)
    sin = jnp.concatenate([jnp.sin(ang), jnp.zeros((CTX_LEN, head_dim // 2), F32)], axis=0)
    return cos, sin


def kernel(x, c, ctx, c_ctx, ada_w, ada_b, norm_mix_g, norm_ffn_g, ffn_w_gate_up, ffn_w_down,
           da_w_qkv, da_w_o, da_lambda, da_subln_g,
           sg_w_in, sg_v_g, sg_w_s, sg_b_s, sg_w_out,
           ret_w_q, ret_w_k, ret_w_v, ret_w_g, ret_w_o, ret_decay, final_norm_g):
    xs = jnp.concatenate([x, ctx], axis=1).reshape(ROWS, D_MODEL)
    cvec = jnp.concatenate([c, c_ctx[None, :], jnp.zeros((MOD_ROWS - BATCH - 1, D_MODEL), F32)], axis=0)
    mods = _ada_tables(cvec, ada_w, ada_b)

    da_cos, da_sin = _rope_tables(DA_HEAD_DIM)
    da_tables = (jnp.concatenate([da_cos, da_cos], axis=1), jnp.concatenate([-da_sin, da_sin], axis=1))
    ret_tables = _rope_tables(RET_KEY_DIM)
    da_q_scale = DA_HEAD_DIM ** -0.5 * math.log2(math.e)

    for i in range(DEPTH):
        kind = i % N_MIXERS
        j = i // N_MIXERS
        h = _normmod(xs, norm_mix_g[i], mods, i, 0, 1)
        if kind == 0:
            lambda_init = 0.8 - 0.6 * math.exp(-0.3 * i)
            lv = da_lambda[j].astype(F32)
            lam = (jnp.exp(jnp.sum(lv[0] * lv[1])) - jnp.exp(jnp.sum(lv[2] * lv[3])) + lambda_init).reshape(1)
            qkv = _proj(h, da_w_qkv, j, epi="rope128", out_dtype=BF16, tables=da_tables,
                        col_scale=da_q_scale, name="da_qkv")
            attn = functools.partial(_diff_attn, qkv, lam, da_subln_g[j], 1.0 - lambda_init)
            y = attn(q_tiles=SEQ // CTX_LEN, q_tile0=0, kv_rows=ROWS_B, kv_blk0=0)
            y = attn(q_tiles=1, q_tile0=SEQ // CTX_LEN, kv_rows=CTX_LEN, kv_blk0=SEQ // CTX_LEN, prev=y)
            xs = _resid(y, da_w_o, j, xs, mods, i, 2, name="da_out")
        elif kind == 1:
            z = _proj(h, sg_w_in, j, epi="gelu", out_dtype=F32, name="sg_in")
            y = _sg_gate(z, sg_v_g[j], sg_w_s[j], sg_b_s[j])
            xs = _resid(y, sg_w_out, j, xs, mods, i, 2, name="sg_out")
        else:
            log_gamma = -jnp.exp(ret_decay[j].astype(F32))
            q = _proj(h, ret_w_q, j, epi="rope256", out_dtype=BF16, tables=ret_tables, name="ret_q")
            k = _proj(h, ret_w_k, j, epi="rope256", out_dtype=BF16, tables=ret_tables,
                      col_scale=RET_KEY_DIM ** -0.5, name="ret_k")
            v = _proj(h, ret_w_v, j, epi="plain", out_dtype=BF16, name="ret_v")
            gp = _proj(h, ret_w_g, j, epi="silu", out_dtype=F32, name="ret_g")
            o_f = _ret_scan(q, k, v, log_gamma[0], backward=False)
            y = _ret_scan(q, k, v, log_gamma[1], backward=True, o_fwd=o_f, gproj=gp)
            xs = _resid(y, ret_w_o, j, xs, mods, i, 2, name="ret_out")
        h = _normmod(xs, norm_ffn_g[i], mods, i, 3, 4)
        act = _glu(h, ffn_w_gate_up, i)
        xs = _resid(act, ffn_w_down, i, xs, mods, i, 5, name="ffn_down")
    return _final_norm(xs, final_norm_g)
```

```python
import functools
import math

import jax
import jax.numpy as jnp
from jax import lax
from jax.experimental import pallas as pl
from jax.experimental.pallas import tpu as pltpu

D_MODEL = 2048
BATCH = 2
SEQ = 4096
DEPTH = 4
GRID_W = 64
CTX_LEN = 256
N_MIXERS = 3
EPS = 1e-6
ROPE_BASE = 10000.0
N_MOD = 6

DA_HEAD_DIM = 128
DA_HEADS = D_MODEL // (2 * DA_HEAD_DIM)
DA_WIDTH = 2 * DA_HEADS * DA_HEAD_DIM
DA_HEAD_W = 2 * DA_HEAD_DIM

SG_CHUNK = 128
SG_GROUP_DIM = 128
SG_WIDTH = D_MODEL
SG_GROUPS = SG_WIDTH // SG_GROUP_DIM

RET_HEADS = D_MODEL // 256
RET_KEY_DIM = D_MODEL // RET_HEADS
RET_VAL_DIM = 2 * D_MODEL // RET_HEADS
RET_CHUNK = 256

FFN_HIDDEN = -((-8 * D_MODEL) // (3 * 256)) * 256

ROWS_B = SEQ + CTX_LEN
ROWS = BATCH * ROWS_B
TM_WIDE = ROWS_B // 2
TM_DEEP = ROWS_B // 4
NORM_TM = CTX_LEN
NORM_STRIP = 16
MOD_ROWS = 8
CTX_MOD_ROW = BATCH

LANE = 128
SUBLANE = 8
MXU_DIM = 256
VMEM_LIMIT_CAP = 60 * 1024 * 1024
VMEM_LIMIT_FLOOR = 40 * 1024 * 1024

BF16 = jnp.bfloat16
F32 = jnp.float32


def _vmem_limit(block_bytes, temp_bytes):
    return int(max(VMEM_LIMIT_FLOOR, min(VMEM_LIMIT_CAP, 2 * block_bytes + temp_bytes + (4 << 20))))


def _nbytes(shape, dtype):
    return math.prod(shape) * jnp.dtype(dtype).itemsize


def _is_latent_rows(tile_idx, tm):
    row = lax.broadcasted_iota(jnp.int32, (tm, 1), 0) + (tile_idx % (ROWS_B // tm)) * tm
    return row < SEQ


def _silu(x):
    return x * (1.0 / (1.0 + jnp.exp(-x)))


def _ada_kernel(c_ref, w_ref, b_ref, o_ref):
    s = _silu(c_ref[...]).astype(BF16)
    acc = jnp.dot(s, w_ref[...].astype(BF16), preferred_element_type=F32)
    o_ref[...] = acc + b_ref[...]


def _ada_tables(cvec, ada_w, ada_b):
    tn = 1024
    n = N_MOD * D_MODEL
    blocks = _nbytes((D_MODEL, tn), F32) + _nbytes((MOD_ROWS, tn), F32) * 2
    return pl.pallas_call(
        _ada_kernel,
        grid=(DEPTH, n // tn),
        in_specs=[
            pl.BlockSpec((MOD_ROWS, D_MODEL), lambda l, j: (0, 0)),
            pl.BlockSpec((None, D_MODEL, tn), lambda l, j: (l, 0, j)),
            pl.BlockSpec((None, 1, tn), lambda l, j: (l, 0, j)),
        ],
        out_specs=pl.BlockSpec((None, MOD_ROWS, tn), lambda l, j: (l, 0, j)),
        out_shape=jax.ShapeDtypeStruct((DEPTH, MOD_ROWS, n), F32),
        compiler_params=pltpu.CompilerParams(
            dimension_semantics=("arbitrary", "arbitrary"),
            vmem_limit_bytes=_vmem_limit(blocks, _nbytes((D_MODEL, tn), BF16))),
        name="ada_tables",
    )(cvec, ada_w, ada_b.reshape(DEPTH, 1, n))


def _mod_spec(layer, which, width, col_of=lambda *idx: 0):
    blocks_per_vec = D_MODEL // width
    return pl.BlockSpec((None, MOD_ROWS, width), lambda *idx: (layer, 0, which * blocks_per_vec + col_of(*idx)))


def _normmod_kernel(x_ref, g_ref, shift_ref, scale_ref, o_ref):
    i = pl.program_id(0)
    tiles_b = ROWS_B // NORM_TM
    row = pl.ds(jnp.where(i % tiles_b == tiles_b - 1, CTX_MOD_ROW, i // tiles_b), 1)
    mul = g_ref[...] * (1.0 + scale_ref[row, :])
    add = shift_ref[row, :]
    for r in range(NORM_TM // NORM_STRIP):
        rows = slice(r * NORM_STRIP, (r + 1) * NORM_STRIP)
        x = x_ref[rows, :]
        inv = lax.rsqrt(jnp.mean(x * x, axis=-1, keepdims=True) + EPS)
        o_ref[rows, :] = (x * inv * mul + add).astype(o_ref.dtype)


def _normmod(xs, gain, mods, layer, shift_idx, scale_idx):
    blocks = _nbytes((NORM_TM, D_MODEL), F32) + _nbytes((NORM_TM, D_MODEL), BF16)
    return pl.pallas_call(
        _normmod_kernel,
        grid=(ROWS // NORM_TM,),
        in_specs=[pl.BlockSpec((NORM_TM, D_MODEL), lambda i: (i, 0)),
                  pl.BlockSpec((1, D_MODEL), lambda i: (0, 0)),
                  _mod_spec(layer, shift_idx, D_MODEL), _mod_spec(layer, scale_idx, D_MODEL)],
        out_specs=pl.BlockSpec((NORM_TM, D_MODEL), lambda i: (i, 0)),
        out_shape=jax.ShapeDtypeStruct((ROWS, D_MODEL), BF16),
        compiler_params=pltpu.CompilerParams(
            dimension_semantics=("arbitrary",),
            vmem_limit_bytes=_vmem_limit(blocks, 0)),
        name="normmod",
    )(xs, gain.reshape(1, D_MODEL), mods, mods)


def _rope_half_roll(acc, cos2, sin2):
    return acc * cos2 + pltpu.roll(acc, DA_HEAD_DIM // 2, 1) * sin2


def _proj_kernel(*refs, epi, tn, col_scale):
    if epi in ("rope128", "rope256"):
        a_ref, w_ref, cos_ref, sin_ref, o_ref = refs
    else:
        a_ref, w_ref, o_ref = refs
    if epi == "rope256":
        cos, sin = cos_ref[...] * col_scale, sin_ref[...] * col_scale
    elif epi == "rope128":
        j = pl.program_id(1)
        tiles_per_part = DA_WIDTH // tn
        rotated = j < 2 * tiles_per_part
        scale = jnp.where(j < tiles_per_part, col_scale, 1.0)
        cos = jnp.where(rotated, cos_ref[...] * scale, 1.0)
        sin = jnp.where(rotated, sin_ref[...] * scale, 0.0)
    a = a_ref[...]
    for c in range(tn // MXU_DIM):
        cols = slice(c * MXU_DIM, (c + 1) * MXU_DIM)
        acc = jnp.dot(a, w_ref[:, cols].astype(BF16), preferred_element_type=F32)
        if epi == "plain":
            out = acc
        elif epi == "silu":
            out = _silu(acc)
        elif epi == "gelu":
            out = 0.5 * acc * (1.0 + lax.erf(acc * (2.0 ** -0.5)))
        elif epi == "rope256":
            x1, x2 = acc[:, :LANE], acc[:, LANE:]
            out = jnp.concatenate([x1 * cos - x2 * sin, x1 * sin + x2 * cos], axis=1)
        elif epi == "rope128":
            out = jnp.concatenate(
                [_rope_half_roll(acc[:, g * LANE:(g + 1) * LANE], cos, sin) for g in range(MXU_DIM // LANE)],
                axis=1)
        else:
            raise ValueError(epi)
        o_ref[:, cols] = out.astype(o_ref.dtype)


def _proj(a, w_stack, layer, *, epi, out_dtype, tables=None, col_scale=1.0, name):
    _, k, n = w_stack.shape
    tm, tn = TM_WIDE, 512
    in_specs = [pl.BlockSpec((tm, k), lambda i, j: (i, 0)),
                pl.BlockSpec((None, k, tn), lambda i, j: (layer, 0, j))]
    args = [a, w_stack]
    if tables is not None:
        in_specs += [pl.BlockSpec((tm, LANE), lambda i, j: (i % (ROWS_B // tm), 0))] * 2
        args += list(tables)
    blocks = (_nbytes((tm, k), BF16) + _nbytes((k, tn), F32) + _nbytes((tm, tn), out_dtype)
              + 2 * _nbytes((tm, LANE), F32))
    temps = _nbytes((k, tn), BF16) + 2 * _nbytes((tm, tn), F32)
    return pl.pallas_call(
        functools.partial(_proj_kernel, epi=epi, tn=tn, col_scale=col_scale),
        grid=(ROWS // tm, n // tn),
        in_specs=in_specs,
        out_specs=pl.BlockSpec((tm, tn), lambda i, j: (i, j)),
        out_shape=jax.ShapeDtypeStruct((ROWS, n), out_dtype),
        compiler_params=pltpu.CompilerParams(
            dimension_semantics=("arbitrary", "arbitrary"),
            vmem_limit_bytes=_vmem_limit(blocks, temps)),
        name=name,
    )(*args)


def _glu_kernel(a_ref, wg_ref, wu_ref, o_ref, *, tn):
    a = a_ref[...]
    for c in range(tn // MXU_DIM):
        cols = slice(c * MXU_DIM, (c + 1) * MXU_DIM)
        gate = jnp.dot(a, wg_ref[:, cols].astype(BF16), preferred_element_type=F32)
        up = jnp.dot(a, wu_ref[:, cols].astype(BF16), preferred_element_type=F32)
        o_ref[:, cols] = (_silu(gate) * up).astype(o_ref.dtype)


def _glu(a, w_stack, layer):
    k = a.shape[1]
    tm, tn = TM_WIDE, 512
    nt = FFN_HIDDEN // tn
    blocks = _nbytes((tm, k), BF16) + 2 * _nbytes((k, tn), F32) + _nbytes((tm, tn), BF16)
    temps = 2 * _nbytes((k, tn), BF16) + 3 * _nbytes((tm, tn), F32)
    return pl.pallas_call(
        functools.partial(_glu_kernel, tn=tn),
        grid=(ROWS // tm, nt),
        in_specs=[pl.BlockSpec((tm, k), lambda i, j: (i, 0)),
                  pl.BlockSpec((None, k, tn), lambda i, j: (layer, 0, j)),
                  pl.BlockSpec((None, k, tn), lambda i, j: (layer, 0, j + nt))],
        out_specs=pl.BlockSpec((tm, tn), lambda i, j: (i, j)),
        out_shape=jax.ShapeDtypeStruct((ROWS, FFN_HIDDEN), BF16),
        compiler_params=pltpu.CompilerParams(
            dimension_semantics=("arbitrary", "arbitrary"),
            vmem_limit_bytes=_vmem_limit(blocks, temps)),
        name="ffn_glu",
    )(a, w_stack, w_stack)


def _resid_kernel(a_ref, w_ref, x_ref, g_ref, o_ref, *, tm, tn):
    i = pl.program_id(0)
    batch_row = pl.ds(i // (ROWS_B // tm), 1)
    latent = _is_latent_rows(i, tm)
    a = a_ref[...]
    for c in range(tn // MXU_DIM):
        cols = slice(c * MXU_DIM, (c + 1) * MXU_DIM)
        acc = jnp.dot(a, w_ref[:, cols].astype(BF16), preferred_element_type=F32)
        gate = jnp.where(latent, g_ref[batch_row, cols], g_ref[CTX_MOD_ROW:CTX_MOD_ROW + 1, cols])
        o_ref[:, cols] = x_ref[:, cols] + gate * acc


def _resid(a, w_stack, layer, xs, mods, mod_layer, gate_idx, *, name):
    k = w_stack.shape[1]
    if k <= D_MODEL:
        tm, tn = TM_WIDE, 512
    elif k <= 2 * D_MODEL:
        tm, tn = TM_DEEP, 512
    else:
        tm, tn = TM_DEEP, 256
    blocks = _nbytes((tm, k), BF16) + _nbytes((k, tn), F32) + 2 * _nbytes((tm, tn), F32)
    temps = _nbytes((k, tn), BF16) + 2 * _nbytes((tm, tn), F32)
    return pl.pallas_call(
        functools.partial(_resid_kernel, tm=tm, tn=tn),
        grid=(ROWS // tm, D_MODEL // tn),
        in_specs=[pl.BlockSpec((tm, k), lambda i, j: (i, 0)),
                  pl.BlockSpec((None, k, tn), lambda i, j: (layer, 0, j)),
                  pl.BlockSpec((tm, tn), lambda i, j: (i, j)),
                  _mod_spec(mod_layer, gate_idx, tn, lambda i, j: j)],
        out_specs=pl.BlockSpec((tm, tn), lambda i, j: (i, j)),
        out_shape=jax.ShapeDtypeStruct((ROWS, D_MODEL), F32),
        compiler_params=pltpu.CompilerParams(
            dimension_semantics=("arbitrary", "arbitrary"),
            vmem_limit_bytes=_vmem_limit(blocks, temps)),
        name=name,
    )(a, w_stack, xs, mods)


def _attn_scores(q_ref, k_ref, s_ref, m_ref, *, keys, rows_a, tq):
    for m in range(2):
        cols = slice(m * DA_HEAD_DIM, (m + 1) * DA_HEAD_DIM)
        qm = q_ref[:, cols]
        mx = None
        for lo in range(keys[0], keys[1], rows_a):
            rows = slice(lo, lo + rows_a)
            s = lax.dot_general(k_ref[rows, cols], qm, (((1,), (1,)), ((), ())), preferred_element_type=F32)
            s_ref[m, rows, :] = s
            part = jnp.max(s.reshape(rows_a // SUBLANE, SUBLANE, tq), axis=0)
            mx = part if mx is None else jnp.maximum(mx, part)
        m_ref[m] = jnp.broadcast_to(jnp.max(mx, axis=0, keepdims=True), (SUBLANE, tq))


def _attn_finish(lam, vt_ref, g_ref, s_ref, m_ref, o_ref, *, keys, tq, out_scale):
    heads = []
    for m in range(2):
        mrow = m_ref[m][0:1, :]
        acc = jnp.zeros((DA_HEAD_W, tq), F32)
        lsum = jnp.zeros((SUBLANE, tq), F32)
        for lo in range(keys[0], keys[1], MXU_DIM):
            rows = slice(lo, lo + MXU_DIM)
            p = jnp.exp2(s_ref[m, rows, :] - mrow)
            lsum = lsum + jnp.sum(p.reshape(MXU_DIM // SUBLANE, SUBLANE, tq), axis=0)
            acc = acc + jnp.dot(vt_ref[:, rows], p.astype(BF16), preferred_element_type=F32)
        heads.append(acc * (1.0 / jnp.sum(lsum, axis=0, keepdims=True)))
    o = heads[0] - lam * heads[1]
    o = o * lax.rsqrt(jnp.mean(o * o, axis=0, keepdims=True) + EPS) * g_ref[...] * out_scale
    o_ref[...] = o.T.astype(o_ref.dtype)


DA_TQ = CTX_LEN
DA_LATENT_TILES = SEQ // DA_TQ
assert DA_LATENT_TILES % 2 == 0


def _diff_attn_kernel(lam_ref, q_ref, k_ref, v_ref, g_ref, o_ref, vt_ref, sa_ref, sb_ref, ma_ref, mb_ref,
                      *, out_scale):
    t = pl.program_id(2)
    lam = lam_ref[0]
    all_keys, ctx_keys = (0, ROWS_B), (SEQ, ROWS_B)
    scores = functools.partial(_attn_scores, q_ref, k_ref, tq=DA_TQ)
    finish = functools.partial(_attn_finish, lam, vt_ref, g_ref, tq=DA_TQ, out_scale=out_scale)

    @pl.when(t == 0)
    def _():
        for lo in range(0, ROWS_B, MXU_DIM):
            vt_ref[:, lo:lo + MXU_DIM] = v_ref[lo:lo + MXU_DIM, :].astype(F32).T.astype(BF16)
        scores(sa_ref, ma_ref, keys=all_keys, rows_a=TM_DEEP)

    @pl.when(jnp.logical_and(jnp.logical_and(t > 0, t < DA_LATENT_TILES), t % 2 == 0))
    def _():
        scores(sa_ref, ma_ref, keys=all_keys, rows_a=TM_DEEP)
        finish(sb_ref, mb_ref, o_ref, keys=all_keys)

    @pl.when(jnp.logical_and(t < DA_LATENT_TILES, t % 2 == 1))
    def _():
        scores(sb_ref, mb_ref, keys=all_keys, rows_a=TM_DEEP)
        finish(sa_ref, ma_ref, o_ref, keys=all_keys)

    @pl.when(t == DA_LATENT_TILES)
    def _():
        scores(sa_ref, ma_ref, keys=ctx_keys, rows_a=CTX_LEN)
        finish(sb_ref, mb_ref, o_ref, keys=all_keys)

    @pl.when(t == DA_LATENT_TILES + 1)
    def _():
        finish(sa_ref, ma_ref, o_ref, keys=ctx_keys)


def _diff_attn(qkv, lam, subln_g, out_scale):
    tiles_b = ROWS_B // DA_TQ
    kcol0 = DA_WIDTH // DA_HEAD_W
    vcol0 = 2 * kcol0
    blocks = 2 * _nbytes((DA_TQ, DA_HEAD_W), BF16) + 2 * _nbytes((ROWS_B, DA_HEAD_W), BF16)
    scratch = (_nbytes((DA_HEAD_W, ROWS_B), BF16) + 4 * _nbytes((ROWS_B, DA_TQ), F32)
               + 4 * _nbytes((SUBLANE, DA_TQ), F32))
    return pl.pallas_call(
        functools.partial(_diff_attn_kernel, out_scale=out_scale),
        grid=(BATCH, DA_HEADS, DA_LATENT_TILES + 2),
        in_specs=[
            pl.BlockSpec(memory_space=pltpu.SMEM),
            pl.BlockSpec((DA_TQ, DA_HEAD_W),
                         lambda b, h, t: (b * tiles_b + jnp.minimum(t, DA_LATENT_TILES), h)),
            pl.BlockSpec((ROWS_B, DA_HEAD_W), lambda b, h, t: (b, kcol0 + h)),
            pl.BlockSpec((ROWS_B, DA_HEAD_W), lambda b, h, t: (b, vcol0 + h)),
            pl.BlockSpec((DA_HEAD_W, 1), lambda b, h, t: (0, 0)),
        ],
        out_specs=pl.BlockSpec((DA_TQ, DA_HEAD_W),
                               lambda b, h, t: (b * tiles_b + jnp.maximum(t - 1, 0), h)),
        out_shape=jax.ShapeDtypeStruct((ROWS, DA_WIDTH), BF16),
        scratch_shapes=[pltpu.VMEM((DA_HEAD_W, ROWS_B), BF16),
                        pltpu.VMEM((2, ROWS_B, DA_TQ), F32), pltpu.VMEM((2, ROWS_B, DA_TQ), F32),
                        pltpu.VMEM((2, SUBLANE, DA_TQ), F32), pltpu.VMEM((2, SUBLANE, DA_TQ), F32)],
        compiler_params=pltpu.CompilerParams(
            dimension_semantics=("arbitrary", "arbitrary", "arbitrary"),
            vmem_limit_bytes=_vmem_limit(blocks, scratch + 4 * _nbytes((TM_DEEP, DA_TQ), F32))),
        name="diff_attn",
    )(lam, qkv, qkv, qkv, subln_g.reshape(DA_HEAD_W, 1))


def _sg_gate_kernel(z_ref, vg_ref, ws_ref, bs_ref, o_ref, *, chunks):
    v = z_ref[:, SG_WIDTH:]
    v = v * lax.rsqrt(jnp.mean(v * v, axis=-1, keepdims=True) + EPS) * vg_ref[...]
    for c in range(chunks):
        r0 = c * SG_CHUNK
        for g in range(SG_GROUPS):
            c0 = g * SG_GROUP_DIM
            vg = v[r0:r0 + SG_CHUNK, c0:c0 + SG_GROUP_DIM].astype(BF16)
            mixed = jnp.dot(ws_ref[g].astype(BF16), vg, preferred_element_type=F32) + bs_ref[g]
            u = z_ref[r0:r0 + SG_CHUNK, c0:c0 + SG_GROUP_DIM]
            o_ref[r0:r0 + SG_CHUNK, c0:c0 + SG_GROUP_DIM] = (u * mixed).astype(o_ref.dtype)


def _sg_gate(z, v_gain, w_s, b_s):
    chunks = 2
    tm = chunks * SG_CHUNK
    bs_b = jnp.broadcast_to(b_s[:, :, None], (SG_GROUPS, SG_CHUNK, SG_GROUP_DIM))
    blocks = (_nbytes((tm, 2 * SG_WIDTH), F32) + _nbytes((tm, SG_WIDTH), BF16)
              + 2 * _nbytes((SG_GROUPS, SG_CHUNK, SG_CHUNK), F32))
    return pl.pallas_call(
        functools.partial(_sg_gate_kernel, chunks=chunks),
        grid=(ROWS // tm,),
        in_specs=[pl.BlockSpec((tm, 2 * SG_WIDTH), lambda i: (i, 0)),
                  pl.BlockSpec((1, SG_WIDTH), lambda i: (0, 0)),
                  pl.BlockSpec((SG_GROUPS, SG_CHUNK, SG_CHUNK), lambda i: (0, 0, 0)),
                  pl.BlockSpec((SG_GROUPS, SG_CHUNK, SG_GROUP_DIM), lambda i: (0, 0, 0))],
        out_specs=pl.BlockSpec((tm, SG_WIDTH), lambda i: (i, 0)),
        out_shape=jax.ShapeDtypeStruct((ROWS, SG_WIDTH), BF16),
        compiler_params=pltpu.CompilerParams(
            dimension_semantics=("arbitrary",),
            vmem_limit_bytes=_vmem_limit(blocks, 4 * _nbytes((tm, SG_WIDTH), F32))),
        name="sg_gate",
    )(z, v_gain.reshape(1, SG_WIDTH), w_s, bs_b)


def _ret_kernel(*refs, backward):
    if backward:
        g_ref, q_ref, k_ref, v_ref, dec_ref, xi_ref, zeta_ref, of_ref, gp_ref, o_ref, s_ref = refs
    else:
        g_ref, q_ref, k_ref, v_ref, dec_ref, xi_ref, zeta_ref, o_ref, s_ref = refs

    @pl.when(pl.program_id(1) == 0)
    def _():
        s_ref[...] = jnp.zeros_like(s_ref)

    for h in range(RET_HEADS):
        kc = slice(h * RET_KEY_DIM, (h + 1) * RET_KEY_DIM)
        vc = slice(h * RET_VAL_DIM, (h + 1) * RET_VAL_DIM)
        qh = q_ref[:, kc]
        kh = k_ref[:, kc]
        vh = v_ref[:, vc]
        scores = lax.dot_general(qh, kh, (((1,), (1,)), ((), ())), preferred_element_type=F32) * dec_ref[h]
        state = s_ref[h]
        o = jnp.dot(scores.astype(BF16), vh, preferred_element_type=F32)
        cross = jnp.dot(qh, state.astype(BF16), preferred_element_type=F32)
        xi = xi_ref[h]
        o = o + jnp.concatenate(
            [cross[:, e * LANE:(e + 1) * LANE] * xi for e in range(RET_VAL_DIM // LANE)], axis=1)
        zeta = zeta_ref[h]
        kz = jnp.concatenate(
            [kh[:, e * LANE:(e + 1) * LANE].astype(F32) * zeta for e in range(RET_KEY_DIM // LANE)], axis=1)
        kz_t = kz.T.astype(BF16)
        s_ref[h] = g_ref[h] * state + jnp.dot(kz_t, vh, preferred_element_type=F32)
        if backward:
            o = o + of_ref[:, vc]
            o = o * lax.rsqrt(jnp.mean(o * o, axis=-1, keepdims=True) + EPS)
            o_ref[:, vc] = (gp_ref[:, vc] * o).astype(o_ref.dtype)
        else:
            o_ref[:, vc] = o


def _ret_consts(log_gamma, backward):
    pos = jnp.arange(RET_CHUNK, dtype=F32)
    dist = pos[:, None] - pos[None, :]
    if backward:
        dist = -dist
    lg = log_gamma[:, None, None]
    decay = jnp.where(dist >= 0, jnp.exp(lg * jnp.maximum(dist, 0.0)), 0.0)
    pos_in_scan = (RET_CHUNK - 1.0 - pos) if backward else pos
    xi = jnp.exp(log_gamma[:, None] * (pos_in_scan + 1.0))
    zeta = jnp.exp(log_gamma[:, None] * (RET_CHUNK - 1.0 - pos_in_scan))
    g_chunk = jnp.exp(log_gamma * RET_CHUNK)
    bcast = lambda t: jnp.broadcast_to(t[:, :, None], (RET_HEADS, RET_CHUNK, LANE))
    return g_chunk, decay, bcast(xi), bcast(zeta)


def _ret_scan(q, k, v, log_gamma, *, backward, o_fwd=None, gproj=None):
    n_chunks = ROWS_B // RET_CHUNK
    x_chunks = SEQ // RET_CHUNK
    g_chunk, decay, xi, zeta = _ret_consts(log_gamma, backward)
    if backward:
        chunk_of = lambda b, t: (b * n_chunks + (n_chunks - 1 - t), 0)
    else:
        chunk_of = lambda b, t: (b * n_chunks + (t + x_chunks) % n_chunks, 0)
    qk_w = RET_HEADS * RET_KEY_DIM
    v_w = RET_HEADS * RET_VAL_DIM
    const_spec = lambda w: pl.BlockSpec((RET_HEADS, RET_CHUNK, w), lambda b, t: (0, 0, 0))
    in_specs = [pl.BlockSpec(memory_space=pltpu.SMEM),
                pl.BlockSpec((RET_CHUNK, qk_w), chunk_of),
                pl.BlockSpec((RET_CHUNK, qk_w), chunk_of),
                pl.BlockSpec((RET_CHUNK, v_w), chunk_of),
                const_spec(RET_CHUNK), const_spec(LANE), const_spec(LANE)]
    args = [g_chunk, q, k, v, decay, xi, zeta]
    blocks = (2 * _nbytes((RET_CHUNK, qk_w), BF16) + _nbytes((RET_CHUNK, v_w), BF16)
              + _nbytes((RET_HEADS, RET_CHUNK, RET_CHUNK + 2 * LANE), F32)
              + _nbytes((RET_CHUNK, v_w), F32))
    if backward:
        in_specs += [pl.BlockSpec((RET_CHUNK, v_w), chunk_of), pl.BlockSpec((RET_CHUNK, v_w), chunk_of)]
        args += [o_fwd, gproj]
        blocks += 2 * _nbytes((RET_CHUNK, v_w), F32)
        out_dtype = BF16
    else:
        out_dtype = F32
    state_bytes = _nbytes((RET_HEADS, RET_KEY_DIM, RET_VAL_DIM), F32)
    return pl.pallas_call(
        functools.partial(_ret_kernel, backward=backward),
        grid=(BATCH, n_chunks),
        in_specs=in_specs,
        out_specs=pl.BlockSpec((RET_CHUNK, v_w), chunk_of),
        out_shape=jax.ShapeDtypeStruct((ROWS, v_w), out_dtype),
        scratch_shapes=[pltpu.VMEM((RET_HEADS, RET_KEY_DIM, RET_VAL_DIM), F32)],
        compiler_params=pltpu.CompilerParams(
            dimension_semantics=("arbitrary", "arbitrary"),
            vmem_limit_bytes=_vmem_limit(blocks, state_bytes + (8 << 20))),
        name="ret_bwd" if backward else "ret_fwd",
    )(*args)


def _final_norm_kernel(x_ref, g_ref, o_ref):
    x = x_ref[...]
    o_ref[...] = x * lax.rsqrt(jnp.mean(x * x, axis=-1, keepdims=True) + EPS) * g_ref[...]


def _final_norm(xs, gain):
    blocks = 2 * _nbytes((NORM_TM, D_MODEL), F32)
    return pl.pallas_call(
        _final_norm_kernel,
        grid=(BATCH, SEQ // NORM_TM),
        in_specs=[pl.BlockSpec((NORM_TM, D_MODEL), lambda b, t: (b * (ROWS_B // NORM_TM) + t, 0)),
                  pl.BlockSpec((1, D_MODEL), lambda b, t: (0, 0))],
        out_specs=pl.BlockSpec((None, NORM_TM, D_MODEL), lambda b, t: (b, t, 0)),
        out_shape=jax.ShapeDtypeStruct((BATCH, SEQ, D_MODEL), F32),
        compiler_params=pltpu.CompilerParams(
            dimension_semantics=("arbitrary", "arbitrary"),
            vmem_limit_bytes=_vmem_limit(blocks, 2 * _nbytes((NORM_TM, D_MODEL), F32))),
        name="final_norm",
    )(xs, gain.reshape(1, D_MODEL))


def _rope_tables(head_dim):
    rows = SEQ // GRID_W
    row = jnp.broadcast_to(jnp.arange(rows)[:, None], (rows, GRID_W)).reshape(-1).astype(F32)
    col = jnp.broadcast_to(jnp.arange(GRID_W)[None, :], (rows, GRID_W)).reshape(-1).astype(F32)
    n_freq = head_dim // 4
    inv_freq = ROPE_BASE ** (-jnp.arange(n_freq, dtype=F32) / n_freq)
    ang = jnp.concatenate([row[:, None] * inv_freq, col[:, None] * inv_freq], axis=-1)
    cos = jnp.concatenate([jnp.cos(ang), jnp.ones((CTX_LEN, head_dim // 2), F32)], axis=0)
    sin = jnp.concatenate([jnp.sin(ang), jnp.zeros((CTX_LEN, head_dim // 2), F32)], axis=0)
    return cos, sin


def kernel(x, c, ctx, c_ctx, ada_w, ada_b, norm_mix_g, norm_ffn_g, ffn_w_gate_up, ffn_w_down,
           da_w_qkv, da_w_o, da_lambda, da_subln_g,
           sg_w_in, sg_v_g, sg_w_s, sg_b_s, sg_w_out,
           ret_w_q, ret_w_k, ret_w_v, ret_w_g, ret_w_o, ret_decay, final_norm_g):
    xs = jnp.concatenate([x, ctx], axis=1).reshape(ROWS, D_MODEL)
    cvec = jnp.concatenate([c, c_ctx[None, :], jnp.zeros((MOD_ROWS - BATCH - 1, D_MODEL), F32)], axis=0)
    mods = _ada_tables(cvec, ada_w, ada_b)

    da_cos, da_sin = _rope_tables(DA_HEAD_DIM)
    da_tables = (jnp.concatenate([da_cos, da_cos], axis=1), jnp.concatenate([-da_sin, da_sin], axis=1))
    ret_tables = _rope_tables(RET_KEY_DIM)
    da_q_scale = DA_HEAD_DIM ** -0.5 * math.log2(math.e)

    for i in range(DEPTH):
        kind = i % N_MIXERS
        j = i // N_MIXERS
        h = _normmod(xs, norm_mix_g[i], mods, i, 0, 1)
        if kind == 0:
            lambda_init = 0.8 - 0.6 * math.exp(-0.3 * i)
            lv = da_lambda[j].astype(F32)
            lam = (jnp.exp(jnp.sum(lv[0] * lv[1])) - jnp.exp(jnp.sum(lv[2] * lv[3])) + lambda_init).reshape(1)
            qkv = _proj(h, da_w_qkv, j, epi="rope128", out_dtype=BF16, tables=da_tables,
                        col_scale=da_q_scale, name="da_qkv")
            y = _diff_attn(qkv, lam, da_subln_g[j], 1.0 - lambda_init)
            xs = _resid(y, da_w_o, j, xs, mods, i, 2, name="da_out")
        elif kind == 1:
            z = _proj(h, sg_w_in, j, epi="gelu", out_dtype=F32, name="sg_in")
            y = _sg_gate(z, sg_v_g[j], sg_w_s[j], sg_b_s[j])
            xs = _resid(y, sg_w_out, j, xs, mods, i, 2, name="sg_out")
        else:
            log_gamma = -jnp.exp(ret_decay[j].astype(F32))
            q = _proj(h, ret_w_q, j, epi="rope256", out_dtype=BF16, tables=ret_tables, name="ret_q")
            k = _proj(h, ret_w_k, j, epi="rope256", out_dtype=BF16, tables=ret_tables,
                      col_scale=RET_KEY_DIM ** -0.5, name="ret_k")
            v = _proj(h, ret_w_v, j, epi="plain", out_dtype=BF16, name="ret_v")
            gp = _proj(h, ret_w_g, j, epi="silu", out_dtype=F32, name="ret_g")
            o_f = _ret_scan(q, k, v, log_gamma[0], backward=False)
            y = _ret_scan(q, k, v, log_gamma[1], backward=True, o_fwd=o_f, gproj=gp)
            xs = _resid(y, ret_w_o, j, xs, mods, i, 2, name="ret_out")
        h = _normmod(xs, norm_ffn_g[i], mods, i, 3, 4)
        act = _glu(h, ffn_w_gate_up, i)
        xs = _resid(act, ffn_w_down, i, xs, mods, i, 5, name="ffn_down")
    return _final_norm(xs, final_norm_g)
```

```python
import functools
import math

import jax
import jax.numpy as jnp
from jax import lax
from jax.experimental import pallas as pl
from jax.experimental.pallas import tpu as pltpu

D_MODEL = 2048
BATCH = 2
SEQ = 4096
DEPTH = 4
GRID_W = 64
CTX_LEN = 256
N_MIXERS = 3
EPS = 1e-6
ROPE_BASE = 10000.0
N_MOD = 6

DA_HEAD_DIM = 128
DA_HEADS = D_MODEL // (2 * DA_HEAD_DIM)
DA_WIDTH = 2 * DA_HEADS * DA_HEAD_DIM
DA_HEAD_W = 2 * DA_HEAD_DIM

SG_CHUNK = 128
SG_GROUP_DIM = 128
SG_WIDTH = D_MODEL
SG_GROUPS = SG_WIDTH // SG_GROUP_DIM

RET_HEADS = D_MODEL // 256
RET_KEY_DIM = D_MODEL // RET_HEADS
RET_VAL_DIM = 2 * D_MODEL // RET_HEADS
RET_CHUNK = 256

FFN_HIDDEN = -((-8 * D_MODEL) // (3 * 256)) * 256

ROWS_B = SEQ + CTX_LEN
ROWS = BATCH * ROWS_B
TM_WIDE = ROWS_B // 2
TM_DEEP = ROWS_B // 4
NORM_TM = CTX_LEN
NORM_STRIP = 16
MOD_ROWS = 8
CTX_MOD_ROW = BATCH

LANE = 128
SUBLANE = 8
MXU_DIM = 256
VMEM_LIMIT_CAP = 60 * 1024 * 1024
VMEM_LIMIT_FLOOR = 40 * 1024 * 1024

BF16 = jnp.bfloat16
F32 = jnp.float32


def _vmem_limit(block_bytes, temp_bytes):
    return int(max(VMEM_LIMIT_FLOOR, min(VMEM_LIMIT_CAP, 2 * block_bytes + temp_bytes + (4 << 20))))


def _nbytes(shape, dtype):
    return math.prod(shape) * jnp.dtype(dtype).itemsize


def _is_latent_rows(tile_idx, tm):
    row = lax.broadcasted_iota(jnp.int32, (tm, 1), 0) + (tile_idx % (ROWS_B // tm)) * tm
    return row < SEQ


def _silu(x):
    return x * (1.0 / (1.0 + jnp.exp(-x)))


def _ada_kernel(c_ref, w_ref, b_ref, o_ref):
    s = _silu(c_ref[...]).astype(BF16)
    acc = jnp.dot(s, w_ref[...].astype(BF16), preferred_element_type=F32)
    o_ref[...] = acc + b_ref[...]


def _ada_tables(cvec, ada_w, ada_b):
    tn = 1024
    n = N_MOD * D_MODEL
    blocks = _nbytes((D_MODEL, tn), F32) + _nbytes((MOD_ROWS, tn), F32) * 2
    return pl.pallas_call(
        _ada_kernel,
        grid=(DEPTH, n // tn),
        in_specs=[
            pl.BlockSpec((MOD_ROWS, D_MODEL), lambda l, j: (0, 0)),
            pl.BlockSpec((None, D_MODEL, tn), lambda l, j: (l, 0, j)),
            pl.BlockSpec((None, 1, tn), lambda l, j: (l, 0, j)),
        ],
        out_specs=pl.BlockSpec((None, MOD_ROWS, tn), lambda l, j: (l, 0, j)),
        out_shape=jax.ShapeDtypeStruct((DEPTH, MOD_ROWS, n), F32),
        compiler_params=pltpu.CompilerParams(
            dimension_semantics=("arbitrary", "arbitrary"),
            vmem_limit_bytes=_vmem_limit(blocks, _nbytes((D_MODEL, tn), BF16))),
        name="ada_tables",
    )(cvec, ada_w, ada_b.reshape(DEPTH, 1, n))


def _mod_spec(layer, which, width, col_of=lambda *idx: 0):
    blocks_per_vec = D_MODEL // width
    return pl.BlockSpec((None, MOD_ROWS, width), lambda *idx: (layer, 0, which * blocks_per_vec + col_of(*idx)))


def _normmod_kernel(x_ref, g_ref, shift_ref, scale_ref, o_ref):
    i = pl.program_id(0)
    tiles_b = ROWS_B // NORM_TM
    row = pl.ds(jnp.where(i % tiles_b == tiles_b - 1, CTX_MOD_ROW, i // tiles_b), 1)
    mul = g_ref[...] * (1.0 + scale_ref[row, :])
    add = shift_ref[row, :]
    for r in range(NORM_TM // NORM_STRIP):
        rows = slice(r * NORM_STRIP, (r + 1) * NORM_STRIP)
        x = x_ref[rows, :]
        inv = lax.rsqrt(jnp.mean(x * x, axis=-1, keepdims=True) + EPS)
        o_ref[rows, :] = (x * inv * mul + add).astype(o_ref.dtype)


def _normmod(xs, gain, mods, layer, shift_idx, scale_idx):
    blocks = _nbytes((NORM_TM, D_MODEL), F32) + _nbytes((NORM_TM, D_MODEL), BF16)
    return pl.pallas_call(
        _normmod_kernel,
        grid=(ROWS // NORM_TM,),
        in_specs=[pl.BlockSpec((NORM_TM, D_MODEL), lambda i: (i, 0)),
                  pl.BlockSpec((1, D_MODEL), lambda i: (0, 0)),
                  _mod_spec(layer, shift_idx, D_MODEL), _mod_spec(layer, scale_idx, D_MODEL)],
        out_specs=pl.BlockSpec((NORM_TM, D_MODEL), lambda i: (i, 0)),
        out_shape=jax.ShapeDtypeStruct((ROWS, D_MODEL), BF16),
        compiler_params=pltpu.CompilerParams(
            dimension_semantics=("arbitrary",),
            vmem_limit_bytes=_vmem_limit(blocks, 0)),
        name="normmod",
    )(xs, gain.reshape(1, D_MODEL), mods, mods)


def _rope_half_roll(acc, cos2, sin2):
    return acc * cos2 + pltpu.roll(acc, DA_HEAD_DIM // 2, 1) * sin2


def _proj_chunks(a, w_refs, table_refs, o_ref, *, epi, tn, col_scale):
    if epi == "rope256":
        cos, sin = table_refs[0][...] * col_scale, table_refs[1][...] * col_scale
    elif epi == "rope128":
        j = pl.program_id(1)
        tiles_per_part = DA_WIDTH // tn
        rotated = j < 2 * tiles_per_part
        scale = jnp.where(j < tiles_per_part, col_scale, 1.0)
        cos = jnp.where(rotated, table_refs[0][...] * scale, 1.0)
        sin = jnp.where(rotated, table_refs[1][...] * scale, 0.0)
    for c in range(tn // MXU_DIM):
        cols = slice(c * MXU_DIM, (c + 1) * MXU_DIM)
        acc = jnp.dot(a, w_refs[0][:, cols].astype(BF16), preferred_element_type=F32)
        if epi == "plain":
            out = acc
        elif epi == "silu":
            out = _silu(acc)
        elif epi == "gelu":
            out = 0.5 * acc * (1.0 + lax.erf(acc * (2.0 ** -0.5)))
        elif epi == "glu":
            out = _silu(acc) * jnp.dot(a, w_refs[1][:, cols].astype(BF16), preferred_element_type=F32)
        elif epi == "rope256":
            x1, x2 = acc[:, :LANE], acc[:, LANE:]
            out = jnp.concatenate([x1 * cos - x2 * sin, x1 * sin + x2 * cos], axis=1)
        elif epi == "rope128":
            out = jnp.concatenate(
                [_rope_half_roll(acc[:, g * LANE:(g + 1) * LANE], cos, sin) for g in range(MXU_DIM // LANE)],
                axis=1)
        else:
            raise ValueError(epi)
        o_ref[:, cols] = out.astype(o_ref.dtype)


def _proj_kernel(*refs, epi, tn, col_scale):
    a_ref, w_ref, *table_refs, o_ref = refs
    _proj_chunks(a_ref[...], (w_ref,), table_refs, o_ref, epi=epi, tn=tn, col_scale=col_scale)


def _proj(a, w_stack, layer, *, epi, out_dtype, tables=None, col_scale=1.0, name):
    _, k, n = w_stack.shape
    tm, tn = TM_WIDE, 512
    in_specs = [pl.BlockSpec((tm, k), lambda i, j: (i, 0)),
                pl.BlockSpec((None, k, tn), lambda i, j: (layer, 0, j))]
    args = [a, w_stack]
    if tables is not None:
        in_specs += [pl.BlockSpec((tm, LANE), lambda i, j: (i % (ROWS_B // tm), 0))] * 2
        args += list(tables)
    blocks = (_nbytes((tm, k), BF16) + _nbytes((k, tn), F32) + _nbytes((tm, tn), out_dtype)
              + 2 * _nbytes((tm, LANE), F32))
    temps = _nbytes((k, tn), BF16) + 2 * _nbytes((tm, tn), F32)
    return pl.pallas_call(
        functools.partial(_proj_kernel, epi=epi, tn=tn, col_scale=col_scale),
        grid=(ROWS // tm, n // tn),
        in_specs=in_specs,
        out_specs=pl.BlockSpec((tm, tn), lambda i, j: (i, j)),
        out_shape=jax.ShapeDtypeStruct((ROWS, n), out_dtype),
        compiler_params=pltpu.CompilerParams(
            dimension_semantics=("arbitrary", "arbitrary"),
            vmem_limit_bytes=_vmem_limit(blocks, temps)),
        name=name,
    )(*args)


NP_SLABS = 8
NP_SLAB = TM_WIDE // NP_SLABS
NP_ROW_TILES = ROWS // TM_WIDE


def _norm_slab(x_ref, gain_ref, shift_ref, scale_ref, h_ref, tile, slab):
    tiles_b = ROWS_B // TM_WIDE
    batch_row = pl.ds(tile // tiles_b, 1)
    ctx_row = slice(CTX_MOD_ROW, CTX_MOD_ROW + 1)
    gain = gain_ref[...]
    mul_b, add_b = gain * (1.0 + scale_ref[batch_row, :]), shift_ref[batch_row, :]
    mul_c, add_c = gain * (1.0 + scale_ref[ctx_row, :]), shift_ref[ctx_row, :]
    row0 = (tile % tiles_b) * TM_WIDE + slab * NP_SLAB
    for s in range(NP_SLAB // NORM_STRIP):
        latent = row0 + s * NORM_STRIP < SEQ
        x = x_ref[s * NORM_STRIP:(s + 1) * NORM_STRIP, :]
        inv = lax.rsqrt(jnp.mean(x * x, axis=-1, keepdims=True) + EPS)
        y = x * inv * jnp.where(latent, mul_b, mul_c) + jnp.where(latent, add_b, add_c)
        rows = pl.ds(pl.multiple_of(slab * NP_SLAB + s * NORM_STRIP, NORM_STRIP), NORM_STRIP)
        h_ref[rows, :] = y.astype(h_ref.dtype)


def _nproj_kernel(*refs, epi, tn, col_scale, n_w):
    x_ref, gain_ref, shift_ref, scale_ref = refs[:4]
    w_refs = refs[4:4 + n_w]
    *table_refs, o_ref, ha_ref, hb_ref = refs[4 + n_w:]
    r = pl.program_id(0)
    tile = jnp.minimum(r, NP_ROW_TILES - 1)
    slab = jnp.minimum(pl.program_id(1), NP_SLABS - 1)
    norm = functools.partial(_norm_slab, x_ref, gain_ref, shift_ref, scale_ref)
    proj = functools.partial(_proj_chunks, epi=epi, tn=tn, col_scale=col_scale)

    @pl.when(r == 0)
    def _():
        norm(ha_ref, tile, slab)

    @pl.when(r % 2 == 1)
    def _():
        norm(hb_ref, tile, slab)
        proj(ha_ref[...], w_refs, table_refs, o_ref)

    @pl.when(jnp.logical_and(r > 0, r % 2 == 0))
    def _():
        norm(ha_ref, tile, slab)
        proj(hb_ref[...], w_refs, table_refs, o_ref)


def _nproj(xs, gain, mods, mod_layer, shift_idx, scale_idx, w_stack, layer, *, epi, out_dtype,
           tables=None, col_scale=1.0, name):
    _, k, n = w_stack.shape
    tm, tn = TM_WIDE, 512
    n_w = 2 if epi == "glu" else 1
    n_out = n // n_w
    nt = n_out // tn
    assert nt >= NP_SLABS and k == D_MODEL
    col = lambda r, j: jnp.where(r == 0, 0, j)
    out_tile = lambda r: jnp.maximum(r - 1, 0)
    x_map = lambda r, j: (jnp.minimum(r, NP_ROW_TILES - 1) * NP_SLABS + jnp.minimum(j, NP_SLABS - 1), 0)
    in_specs = [pl.BlockSpec((NP_SLAB, D_MODEL), x_map),
                pl.BlockSpec((1, D_MODEL), lambda r, j: (0, 0)),
                _mod_spec(mod_layer, shift_idx, D_MODEL), _mod_spec(mod_layer, scale_idx, D_MODEL)]
    in_specs += [pl.BlockSpec((None, k, tn), lambda r, j, w=w: (layer, 0, col(r, j) + w * nt)) for w in range(n_w)]
    args = [xs, gain.reshape(1, D_MODEL), mods, mods] + [w_stack] * n_w
    if tables is not None:
        in_specs += [pl.BlockSpec((tm, LANE), lambda r, j: (out_tile(r) % (ROWS_B // tm), 0))] * 2
        args += list(tables)
    blocks = (_nbytes((NP_SLAB, D_MODEL), F32) + n_w * _nbytes((k, tn), F32) + _nbytes((tm, tn), out_dtype)
              + 2 * _nbytes((tm, LANE), F32))
    scratch = 2 * _nbytes((tm, D_MODEL), BF16)
    temps = n_w * _nbytes((k, tn), BF16) + 3 * _nbytes((tm, tn), F32)
    return pl.pallas_call(
        functools.partial(_nproj_kernel, epi=epi, tn=tn, col_scale=col_scale, n_w=n_w),
        grid=(NP_ROW_TILES + 1, nt),
        in_specs=in_specs,
        out_specs=pl.BlockSpec((tm, tn), lambda r, j: (out_tile(r), col(r, j))),
        out_shape=jax.ShapeDtypeStruct((ROWS, n_out), out_dtype),
        scratch_shapes=[pltpu.VMEM((tm, D_MODEL), BF16), pltpu.VMEM((tm, D_MODEL), BF16)],
        compiler_params=pltpu.CompilerParams(
            dimension_semantics=("arbitrary", "arbitrary"),
            vmem_limit_bytes=_vmem_limit(blocks, scratch + temps)),
        name=name,
    )(*args)


def _resid_kernel(a_ref, w_ref, x_ref, g_ref, o_ref, *, tm, tn):
    i = pl.program_id(0)
    batch_row = pl.ds(i // (ROWS_B // tm), 1)
    latent = _is_latent_rows(i, tm)
    a = a_ref[...]
    for c in range(tn // MXU_DIM):
        cols = slice(c * MXU_DIM, (c + 1) * MXU_DIM)
        acc = jnp.dot(a, w_ref[:, cols].astype(BF16), preferred_element_type=F32)
        gate = jnp.where(latent, g_ref[batch_row, cols], g_ref[CTX_MOD_ROW:CTX_MOD_ROW + 1, cols])
        o_ref[:, cols] = x_ref[:, cols] + gate * acc


def _resid(a, w_stack, layer, xs, mods, mod_layer, gate_idx, *, name):
    k = w_stack.shape[1]
    if k <= D_MODEL:
        tm, tn = TM_WIDE, 512
    elif k <= 2 * D_MODEL:
        tm, tn = TM_DEEP, 512
    else:
        tm, tn = TM_DEEP, 256
    blocks = _nbytes((tm, k), BF16) + _nbytes((k, tn), F32) + 2 * _nbytes((tm, tn), F32)
    temps = _nbytes((k, tn), BF16) + 2 * _nbytes((tm, tn), F32)
    return pl.pallas_call(
        functools.partial(_resid_kernel, tm=tm, tn=tn),
        grid=(ROWS // tm, D_MODEL // tn),
        in_specs=[pl.BlockSpec((tm, k), lambda i, j: (i, 0)),
                  pl.BlockSpec((None, k, tn), lambda i, j: (layer, 0, j)),
                  pl.BlockSpec((tm, tn), lambda i, j: (i, j)),
                  _mod_spec(mod_layer, gate_idx, tn, lambda i, j: j)],
        out_specs=pl.BlockSpec((tm, tn), lambda i, j: (i, j)),
        out_shape=jax.ShapeDtypeStruct((ROWS, D_MODEL), F32),
        compiler_params=pltpu.CompilerParams(
            dimension_semantics=("arbitrary", "arbitrary"),
            vmem_limit_bytes=_vmem_limit(blocks, temps)),
        name=name,
    )(a, w_stack, xs, mods)


def _attn_scores(q_ref, k_ref, s_ref, m_ref, *, keys, rows_a, tq):
    for m in range(2):
        cols = slice(m * DA_HEAD_DIM, (m + 1) * DA_HEAD_DIM)
        qm = q_ref[:, cols]
        mx = None
        for lo in range(keys[0], keys[1], rows_a):
            rows = slice(lo, lo + rows_a)
            s = lax.dot_general(k_ref[rows, cols], qm, (((1,), (1,)), ((), ())), preferred_element_type=F32)
            s_ref[m, rows, :] = s
            part = jnp.max(s.reshape(rows_a // SUBLANE, SUBLANE, tq), axis=0)
            mx = part if mx is None else jnp.maximum(mx, part)
        m_ref[m] = jnp.broadcast_to(jnp.max(mx, axis=0, keepdims=True), (SUBLANE, tq))


def _attn_finish(lam, vt_ref, g_ref, s_ref, m_ref, o_ref, *, keys, tq, out_scale):
    heads = []
    for m in range(2):
        mrow = m_ref[m][0:1, :]
        acc = jnp.zeros((DA_HEAD_W, tq), F32)
        lsum = jnp.zeros((SUBLANE, tq), F32)
        for lo in range(keys[0], keys[1], MXU_DIM):
            rows = slice(lo, lo + MXU_DIM)
            p = jnp.exp2(s_ref[m, rows, :] - mrow)
            lsum = lsum + jnp.sum(p.reshape(MXU_DIM // SUBLANE, SUBLANE, tq), axis=0)
            acc = acc + jnp.dot(vt_ref[:, rows], p.astype(BF16), preferred_element_type=F32)
        heads.append(acc * (1.0 / jnp.sum(lsum, axis=0, keepdims=True)))
    o = heads[0] - lam * heads[1]
    o = o * lax.rsqrt(jnp.mean(o * o, axis=0, keepdims=True) + EPS) * g_ref[...] * out_scale
    o_ref[...] = o.T.astype(o_ref.dtype)


DA_TQ = CTX_LEN
DA_LATENT_TILES = SEQ // DA_TQ
assert DA_LATENT_TILES % 2 == 0


def _diff_attn_kernel(lam_ref, q_ref, k_ref, v_ref, g_ref, o_ref, vt_ref, sa_ref, sb_ref, ma_ref, mb_ref,
                      *, out_scale):
    t = pl.program_id(2)
    lam = lam_ref[0]
    all_keys, ctx_keys = (0, ROWS_B), (SEQ, ROWS_B)
    scores = functools.partial(_attn_scores, q_ref, k_ref, tq=DA_TQ)
    finish = functools.partial(_attn_finish, lam, vt_ref, g_ref, tq=DA_TQ, out_scale=out_scale)

    @pl.when(t == 0)
    def _():
        for lo in range(0, ROWS_B, MXU_DIM):
            vt_ref[:, lo:lo + MXU_DIM] = v_ref[lo:lo + MXU_DIM, :].astype(F32).T.astype(BF16)
        scores(sa_ref, ma_ref, keys=all_keys, rows_a=TM_DEEP)

    @pl.when(jnp.logical_and(jnp.logical_and(t > 0, t < DA_LATENT_TILES), t % 2 == 0))
    def _():
        scores(sa_ref, ma_ref, keys=all_keys, rows_a=TM_DEEP)
        finish(sb_ref, mb_ref, o_ref, keys=all_keys)

    @pl.when(jnp.logical_and(t < DA_LATENT_TILES, t % 2 == 1))
    def _():
        scores(sb_ref, mb_ref, keys=all_keys, rows_a=TM_DEEP)
        finish(sa_ref, ma_ref, o_ref, keys=all_keys)

    @pl.when(t == DA_LATENT_TILES)
    def _():
        scores(sa_ref, ma_ref, keys=ctx_keys, rows_a=CTX_LEN)
        finish(sb_ref, mb_ref, o_ref, keys=all_keys)

    @pl.when(t == DA_LATENT_TILES + 1)
    def _():
        finish(sa_ref, ma_ref, o_ref, keys=ctx_keys)


def _diff_attn(qkv, lam, subln_g, out_scale):
    tiles_b = ROWS_B // DA_TQ
    kcol0 = DA_WIDTH // DA_HEAD_W
    vcol0 = 2 * kcol0
    blocks = 2 * _nbytes((DA_TQ, DA_HEAD_W), BF16) + 2 * _nbytes((ROWS_B, DA_HEAD_W), BF16)
    scratch = (_nbytes((DA_HEAD_W, ROWS_B), BF16) + 4 * _nbytes((ROWS_B, DA_TQ), F32)
               + 4 * _nbytes((SUBLANE, DA_TQ), F32))
    return pl.pallas_call(
        functools.partial(_diff_attn_kernel, out_scale=out_scale),
        grid=(BATCH, DA_HEADS, DA_LATENT_TILES + 2),
        in_specs=[
            pl.BlockSpec(memory_space=pltpu.SMEM),
            pl.BlockSpec((DA_TQ, DA_HEAD_W),
                         lambda b, h, t: (b * tiles_b + jnp.minimum(t, DA_LATENT_TILES), h)),
            pl.BlockSpec((ROWS_B, DA_HEAD_W), lambda b, h, t: (b, kcol0 + h)),
            pl.BlockSpec((ROWS_B, DA_HEAD_W), lambda b, h, t: (b, vcol0 + h)),
            pl.BlockSpec((DA_HEAD_W, 1), lambda b, h, t: (0, 0)),
        ],
        out_specs=pl.BlockSpec((DA_TQ, DA_HEAD_W),
                               lambda b, h, t: (b * tiles_b + jnp.maximum(t - 1, 0), h)),
        out_shape=jax.ShapeDtypeStruct((ROWS, DA_WIDTH), BF16),
        scratch_shapes=[pltpu.VMEM((DA_HEAD_W, ROWS_B), BF16),
                        pltpu.VMEM((2, ROWS_B, DA_TQ), F32), pltpu.VMEM((2, ROWS_B, DA_TQ), F32),
                        pltpu.VMEM((2, SUBLANE, DA_TQ), F32), pltpu.VMEM((2, SUBLANE, DA_TQ), F32)],
        compiler_params=pltpu.CompilerParams(
            dimension_semantics=("arbitrary", "arbitrary", "arbitrary"),
            vmem_limit_bytes=_vmem_limit(blocks, scratch + 4 * _nbytes((TM_DEEP, DA_TQ), F32))),
        name="diff_attn",
    )(lam, qkv, qkv, qkv, subln_g.reshape(DA_HEAD_W, 1))


def _sg_gate_kernel(z_ref, vg_ref, ws_ref, bs_ref, o_ref, *, chunks):
    v = z_ref[:, SG_WIDTH:]
    v = v * lax.rsqrt(jnp.mean(v * v, axis=-1, keepdims=True) + EPS) * vg_ref[...]
    for c in range(chunks):
        r0 = c * SG_CHUNK
        for g in range(SG_GROUPS):
            c0 = g * SG_GROUP_DIM
            vg = v[r0:r0 + SG_CHUNK, c0:c0 + SG_GROUP_DIM].astype(BF16)
            mixed = jnp.dot(ws_ref[g].astype(BF16), vg, preferred_element_type=F32) + bs_ref[g]
            u = z_ref[r0:r0 + SG_CHUNK, c0:c0 + SG_GROUP_DIM]
            o_ref[r0:r0 + SG_CHUNK, c0:c0 + SG_GROUP_DIM] = (u * mixed).astype(o_ref.dtype)


def _sg_gate(z, v_gain, w_s, b_s):
    chunks = 2
    tm = chunks * SG_CHUNK
    bs_b = jnp.broadcast_to(b_s[:, :, None], (SG_GROUPS, SG_CHUNK, SG_GROUP_DIM))
    blocks = (_nbytes((tm, 2 * SG_WIDTH), F32) + _nbytes((tm, SG_WIDTH), BF16)
              + 2 * _nbytes((SG_GROUPS, SG_CHUNK, SG_CHUNK), F32))
    return pl.pallas_call(
        functools.partial(_sg_gate_kernel, chunks=chunks),
        grid=(ROWS // tm,),
        in_specs=[pl.BlockSpec((tm, 2 * SG_WIDTH), lambda i: (i, 0)),
                  pl.BlockSpec((1, SG_WIDTH), lambda i: (0, 0)),
                  pl.BlockSpec((SG_GROUPS, SG_CHUNK, SG_CHUNK), lambda i: (0, 0, 0)),
                  pl.BlockSpec((SG_GROUPS, SG_CHUNK, SG_GROUP_DIM), lambda i: (0, 0, 0))],
        out_specs=pl.BlockSpec((tm, SG_WIDTH), lambda i: (i, 0)),
        out_shape=jax.ShapeDtypeStruct((ROWS, SG_WIDTH), BF16),
        compiler_params=pltpu.CompilerParams(
            dimension_semantics=("arbitrary",),
            vmem_limit_bytes=_vmem_limit(blocks, 4 * _nbytes((tm, SG_WIDTH), F32))),
        name="sg_gate",
    )(z, v_gain.reshape(1, SG_WIDTH), w_s, bs_b)


def _ret_kernel(*refs, backward):
    if backward:
        g_ref, q_ref, k_ref, v_ref, dec_ref, xi_ref, zeta_ref, of_ref, gp_ref, o_ref, s_ref = refs
    else:
        g_ref, q_ref, k_ref, v_ref, dec_ref, xi_ref, zeta_ref, o_ref, s_ref = refs

    @pl.when(pl.program_id(1) == 0)
    def _():
        s_ref[...] = jnp.zeros_like(s_ref)

    for h in range(RET_HEADS):
        kc = slice(h * RET_KEY_DIM, (h + 1) * RET_KEY_DIM)
        vc = slice(h * RET_VAL_DIM, (h + 1) * RET_VAL_DIM)
        qh = q_ref[:, kc]
        kh = k_ref[:, kc]
        vh = v_ref[:, vc]
        scores = lax.dot_general(qh, kh, (((1,), (1,)), ((), ())), preferred_element_type=F32) * dec_ref[h]
        state = s_ref[h]
        o = jnp.dot(scores.astype(BF16), vh, preferred_element_type=F32)
        cross = jnp.dot(qh, state.astype(BF16), preferred_element_type=F32)
        xi = xi_ref[h]
        o = o + jnp.concatenate(
            [cross[:, e * LANE:(e + 1) * LANE] * xi for e in range(RET_VAL_DIM // LANE)], axis=1)
        zeta = zeta_ref[h]
        kz = jnp.concatenate(
            [kh[:, e * LANE:(e + 1) * LANE].astype(F32) * zeta for e in range(RET_KEY_DIM // LANE)], axis=1)
        kz_t = kz.T.astype(BF16)
        s_ref[h] = g_ref[h] * state + jnp.dot(kz_t, vh, preferred_element_type=F32)
        if backward:
            o = o + of_ref[:, vc]
            o = o * lax.rsqrt(jnp.mean(o * o, axis=-1, keepdims=True) + EPS)
            o_ref[:, vc] = (gp_ref[:, vc] * o).astype(o_ref.dtype)
        else:
            o_ref[:, vc] = o


def _ret_consts(log_gamma, backward):
    pos = jnp.arange(RET_CHUNK, dtype=F32)
    dist = pos[:, None] - pos[None, :]
    if backward:
        dist = -dist
    lg = log_gamma[:, None, None]
    decay = jnp.where(dist >= 0, jnp.exp(lg * jnp.maximum(dist, 0.0)), 0.0)
    pos_in_scan = (RET_CHUNK - 1.0 - pos) if backward else pos
    xi = jnp.exp(log_gamma[:, None] * (pos_in_scan + 1.0))
    zeta = jnp.exp(log_gamma[:, None] * (RET_CHUNK - 1.0 - pos_in_scan))
    g_chunk = jnp.exp(log_gamma * RET_CHUNK)
    bcast = lambda t: jnp.broadcast_to(t[:, :, None], (RET_HEADS, RET_CHUNK, LANE))
    return g_chunk, decay, bcast(xi), bcast(zeta)


def _ret_scan(q, k, v, log_gamma, *, backward, o_fwd=None, gproj=None):
    n_chunks = ROWS_B // RET_CHUNK
    x_chunks = SEQ // RET_CHUNK
    g_chunk, decay, xi, zeta = _ret_consts(log_gamma, backward)
    if backward:
        chunk_of = lambda b, t: (b * n_chunks + (n_chunks - 1 - t), 0)
    else:
        chunk_of = lambda b, t: (b * n_chunks + (t + x_chunks) % n_chunks, 0)
    qk_w = RET_HEADS * RET_KEY_DIM
    v_w = RET_HEADS * RET_VAL_DIM
    const_spec = lambda w: pl.BlockSpec((RET_HEADS, RET_CHUNK, w), lambda b, t: (0, 0, 0))
    in_specs = [pl.BlockSpec(memory_space=pltpu.SMEM),
                pl.BlockSpec((RET_CHUNK, qk_w), chunk_of),
                pl.BlockSpec((RET_CHUNK, qk_w), chunk_of),
                pl.BlockSpec((RET_CHUNK, v_w), chunk_of),
                const_spec(RET_CHUNK), const_spec(LANE), const_spec(LANE)]
    args = [g_chunk, q, k, v, decay, xi, zeta]
    blocks = (2 * _nbytes((RET_CHUNK, qk_w), BF16) + _nbytes((RET_CHUNK, v_w), BF16)
              + _nbytes((RET_HEADS, RET_CHUNK, RET_CHUNK + 2 * LANE), F32)
              + _nbytes((RET_CHUNK, v_w), F32))
    if backward:
        in_specs += [pl.BlockSpec((RET_CHUNK, v_w), chunk_of), pl.BlockSpec((RET_CHUNK, v_w), chunk_of)]
        args += [o_fwd, gproj]
        blocks += 2 * _nbytes((RET_CHUNK, v_w), F32)
        out_dtype = BF16
    else:
        out_dtype = F32
    state_bytes = _nbytes((RET_HEADS, RET_KEY_DIM, RET_VAL_DIM), F32)
    return pl.pallas_call(
        functools.partial(_ret_kernel, backward=backward),
        grid=(BATCH, n_chunks),
        in_specs=in_specs,
        out_specs=pl.BlockSpec((RET_CHUNK, v_w), chunk_of),
        out_shape=jax.ShapeDtypeStruct((ROWS, v_w), out_dtype),
        scratch_shapes=[pltpu.VMEM((RET_HEADS, RET_KEY_DIM, RET_VAL_DIM), F32)],
        compiler_params=pltpu.CompilerParams(
            dimension_semantics=("arbitrary", "arbitrary"),
            vmem_limit_bytes=_vmem_limit(blocks, state_bytes + (8 << 20))),
        name="ret_bwd" if backward else "ret_fwd",
    )(*args)


def _final_norm_kernel(x_ref, g_ref, o_ref):
    x = x_ref[...]
    o_ref[...] = x * lax.rsqrt(jnp.mean(x * x, axis=-1, keepdims=True) + EPS) * g_ref[...]


def _final_norm(xs, gain):
    blocks = 2 * _nbytes((NORM_TM, D_MODEL), F32)
    return pl.pallas_call(
        _final_norm_kernel,
        grid=(BATCH, SEQ // NORM_TM),
        in_specs=[pl.BlockSpec((NORM_TM, D_MODEL), lambda b, t: (b * (ROWS_B // NORM_TM) + t, 0)),
                  pl.BlockSpec((1, D_MODEL), lambda b, t: (0, 0))],
        out_specs=pl.BlockSpec((None, NORM_TM, D_MODEL), lambda b, t: (b, t, 0)),
        out_shape=jax.ShapeDtypeStruct((BATCH, SEQ, D_MODEL), F32),
        compiler_params=pltpu.CompilerParams(
            dimension_semantics=("arbitrary", "arbitrary"),
            vmem_limit_bytes=_vmem_limit(blocks, 2 * _nbytes((NORM_TM, D_MODEL), F32))),
        name="final_norm",
    )(xs, gain.reshape(1, D_MODEL))


def _rope_tables(head_dim):
    rows = SEQ // GRID_W
    row = jnp.broadcast_to(jnp.arange(rows)[:, None], (rows, GRID_W)).reshape(-1).astype(F32)
    col = jnp.broadcast_to(jnp.arange(GRID_W)[None, :], (rows, GRID_W)).reshape(-1).astype(F32)
    n_freq = head_dim // 4
    inv_freq = ROPE_BASE ** (-jnp.arange(n_freq, dtype=F32) / n_freq)
    ang = jnp.concatenate([row[:, None] * inv_freq, col[:, None] * inv_freq], axis=-1)
    cos = jnp.concatenate([jnp.cos(ang), jnp.ones((CTX_LEN, head_dim // 2), F32)], axis=0)
    sin = jnp.concatenate([jnp.sin(ang), jnp.zeros((CTX_LEN, head_dim // 2), F32)], axis=0)
    return cos, sin


def kernel(x, c, ctx, c_ctx, ada_w, ada_b, norm_mix_g, norm_ffn_g, ffn_w_gate_up, ffn_w_down,
           da_w_qkv, da_w_o, da_lambda, da_subln_g,
           sg_w_in, sg_v_g, sg_w_s, sg_b_s, sg_w_out,
           ret_w_q, ret_w_k, ret_w_v, ret_w_g, ret_w_o, ret_decay, final_norm_g):
    xs = jnp.concatenate([x, ctx], axis=1).reshape(ROWS, D_MODEL)
    cvec = jnp.concatenate([c, c_ctx[None, :], jnp.zeros((MOD_ROWS - BATCH - 1, D_MODEL), F32)], axis=0)
    mods = _ada_tables(cvec, ada_w, ada_b)

    da_cos, da_sin = _rope_tables(DA_HEAD_DIM)
    da_tables = (jnp.concatenate([da_cos, da_cos], axis=1), jnp.concatenate([-da_sin, da_sin], axis=1))
    ret_tables = _rope_tables(RET_KEY_DIM)
    da_q_scale = DA_HEAD_DIM ** -0.5 * math.log2(math.e)

    for i in range(DEPTH):
        kind = i % N_MIXERS
        j = i // N_MIXERS
        mix_norm = (xs, norm_mix_g[i], mods, i, 0, 1)
        if kind == 0:
            lambda_init = 0.8 - 0.6 * math.exp(-0.3 * i)
            lv = da_lambda[j].astype(F32)
            lam = (jnp.exp(jnp.sum(lv[0] * lv[1])) - jnp.exp(jnp.sum(lv[2] * lv[3])) + lambda_init).reshape(1)
            qkv = _proj(_normmod(*mix_norm), da_w_qkv, j, epi="rope128", out_dtype=BF16, tables=da_tables,
                        col_scale=da_q_scale, name="da_qkv")
            y = _diff_attn(qkv, lam, da_subln_g[j], 1.0 - lambda_init)
            xs = _resid(y, da_w_o, j, xs, mods, i, 2, name="da_out")
        elif kind == 1:
            z = _nproj(*mix_norm, sg_w_in, j, epi="gelu", out_dtype=F32, name="sg_in")
            y = _sg_gate(z, sg_v_g[j], sg_w_s[j], sg_b_s[j])
            xs = _resid(y, sg_w_out, j, xs, mods, i, 2, name="sg_out")
        else:
            h = _normmod(*mix_norm)
            log_gamma = -jnp.exp(ret_decay[j].astype(F32))
            q = _proj(h, ret_w_q, j, epi="rope256", out_dtype=BF16, tables=ret_tables, name="ret_q")
            k = _proj(h, ret_w_k, j, epi="rope256", out_dtype=BF16, tables=ret_tables,
                      col_scale=RET_KEY_DIM ** -0.5, name="ret_k")
            v = _proj(h, ret_w_v, j, epi="plain", out_dtype=BF16, name="ret_v")
            gp = _proj(h, ret_w_g, j, epi="silu", out_dtype=F32, name="ret_g")
            o_f = _ret_scan(q, k, v, log_gamma[0], backward=False)
            y = _ret_scan(q, k, v, log_gamma[1], backward=True, o_fwd=o_f, gproj=gp)
            xs = _resid(y, ret_w_o, j, xs, mods, i, 2, name="ret_out")
        act = _nproj(xs, norm_ffn_g[i], mods, i, 3, 4, ffn_w_gate_up, i, epi="glu", out_dtype=BF16,
                     name="ffn_glu")
        xs = _resid(act, ffn_w_down, i, xs, mods, i, 5, name="ffn_down")
    return _final_norm(xs, final_norm_g)
```

```python
import functools
import math

import jax
import jax.numpy as jnp
from jax import lax
from jax.experimental import pallas as pl
from jax.experimental.pallas import tpu as pltpu

D_MODEL = 2048
BATCH = 2
SEQ = 4096
DEPTH = 4
GRID_W = 64
CTX_LEN = 256
N_MIXERS = 3
EPS = 1e-6
ROPE_BASE = 10000.0
N_MOD = 6

DA_HEAD_DIM = 128
DA_HEADS = D_MODEL // (2 * DA_HEAD_DIM)
DA_WIDTH = 2 * DA_HEADS * DA_HEAD_DIM
DA_HEAD_W = 2 * DA_HEAD_DIM

SG_CHUNK = 128
SG_GROUP_DIM = 128
SG_WIDTH = D_MODEL
SG_GROUPS = SG_WIDTH // SG_GROUP_DIM

RET_HEADS = D_MODEL // 256
RET_KEY_DIM = D_MODEL // RET_HEADS
RET_VAL_DIM = 2 * D_MODEL // RET_HEADS
RET_CHUNK = 256

FFN_HIDDEN = -((-8 * D_MODEL) // (3 * 256)) * 256

ROWS_B = SEQ + CTX_LEN
ROWS = BATCH * ROWS_B
TM_WIDE = ROWS_B // 2
TM_DEEP = ROWS_B // 4
NORM_TM = CTX_LEN
NORM_STRIP = 16
MOD_ROWS = 8
CTX_MOD_ROW = BATCH

LANE = 128
SUBLANE = 8
MXU_DIM = 256
VMEM_LIMIT_CAP = 60 * 1024 * 1024
VMEM_LIMIT_FLOOR = 40 * 1024 * 1024

BF16 = jnp.bfloat16
F32 = jnp.float32


def _vmem_limit(block_bytes, temp_bytes):
    return int(max(VMEM_LIMIT_FLOOR, min(VMEM_LIMIT_CAP, 2 * block_bytes + temp_bytes + (4 << 20))))


def _nbytes(shape, dtype):
    return math.prod(shape) * jnp.dtype(dtype).itemsize


def _is_latent_rows(tile_idx, tm):
    row = lax.broadcasted_iota(jnp.int32, (tm, 1), 0) + (tile_idx % (ROWS_B // tm)) * tm
    return row < SEQ


def _silu(x):
    return x * (1.0 / (1.0 + jnp.exp(-x)))


def _ada_kernel(c_ref, w_ref, b_ref, o_ref):
    s = _silu(c_ref[...]).astype(BF16)
    acc = jnp.dot(s, w_ref[...].astype(BF16), preferred_element_type=F32)
    o_ref[...] = acc + b_ref[...]


def _ada_tables(cvec, ada_w, ada_b):
    tn = 1024
    n = N_MOD * D_MODEL
    blocks = _nbytes((D_MODEL, tn), F32) + _nbytes((MOD_ROWS, tn), F32) * 2
    return pl.pallas_call(
        _ada_kernel,
        grid=(DEPTH, n // tn),
        in_specs=[
            pl.BlockSpec((MOD_ROWS, D_MODEL), lambda l, j: (0, 0)),
            pl.BlockSpec((None, D_MODEL, tn), lambda l, j: (l, 0, j)),
            pl.BlockSpec((None, 1, tn), lambda l, j: (l, 0, j)),
        ],
        out_specs=pl.BlockSpec((None, MOD_ROWS, tn), lambda l, j: (l, 0, j)),
        out_shape=jax.ShapeDtypeStruct((DEPTH, MOD_ROWS, n), F32),
        compiler_params=pltpu.CompilerParams(
            dimension_semantics=("arbitrary", "arbitrary"),
            vmem_limit_bytes=_vmem_limit(blocks, _nbytes((D_MODEL, tn), BF16))),
        name="ada_tables",
    )(cvec, ada_w, ada_b.reshape(DEPTH, 1, n))


def _mod_spec(layer, which, width, col_of=lambda *idx: 0):
    blocks_per_vec = D_MODEL // width
    return pl.BlockSpec((None, MOD_ROWS, width), lambda *idx: (layer, 0, which * blocks_per_vec + col_of(*idx)))


def _normmod_kernel(x_ref, g_ref, shift_ref, scale_ref, o_ref):
    i = pl.program_id(0)
    tiles_b = ROWS_B // NORM_TM
    row = pl.ds(jnp.where(i % tiles_b == tiles_b - 1, CTX_MOD_ROW, i // tiles_b), 1)
    mul = g_ref[...] * (1.0 + scale_ref[row, :])
    add = shift_ref[row, :]
    for r in range(NORM_TM // NORM_STRIP):
        rows = slice(r * NORM_STRIP, (r + 1) * NORM_STRIP)
        x = x_ref[rows, :]
        inv = lax.rsqrt(jnp.mean(x * x, axis=-1, keepdims=True) + EPS)
        o_ref[rows, :] = (x * inv * mul + add).astype(o_ref.dtype)


def _normmod(xs, gain, mods, layer, shift_idx, scale_idx):
    blocks = _nbytes((NORM_TM, D_MODEL), F32) + _nbytes((NORM_TM, D_MODEL), BF16)
    return pl.pallas_call(
        _normmod_kernel,
        grid=(ROWS // NORM_TM,),
        in_specs=[pl.BlockSpec((NORM_TM, D_MODEL), lambda i: (i, 0)),
                  pl.BlockSpec((1, D_MODEL), lambda i: (0, 0)),
                  _mod_spec(layer, shift_idx, D_MODEL), _mod_spec(layer, scale_idx, D_MODEL)],
        out_specs=pl.BlockSpec((NORM_TM, D_MODEL), lambda i: (i, 0)),
        out_shape=jax.ShapeDtypeStruct((ROWS, D_MODEL), BF16),
        compiler_params=pltpu.CompilerParams(
            dimension_semantics=("arbitrary",),
            vmem_limit_bytes=_vmem_limit(blocks, 0)),
        name="normmod",
    )(xs, gain.reshape(1, D_MODEL), mods, mods)


def _rope_half_roll(acc, cos2, sin2):
    return acc * cos2 + pltpu.roll(acc, DA_HEAD_DIM // 2, 1) * sin2


def _proj_chunks(a, w_refs, table_refs, o_ref, *, epi, tn, col_scale):
    if epi == "rope256":
        cos, sin = table_refs[0][...] * col_scale, table_refs[1][...] * col_scale
    elif epi == "rope128":
        j = pl.program_id(1)
        tiles_per_part = DA_WIDTH // tn
        rotated = j < 2 * tiles_per_part
        scale = jnp.where(j < tiles_per_part, col_scale, 1.0)
        cos = jnp.where(rotated, table_refs[0][...] * scale, 1.0)
        sin = jnp.where(rotated, table_refs[1][...] * scale, 0.0)
    for c in range(tn // MXU_DIM):
        cols = slice(c * MXU_DIM, (c + 1) * MXU_DIM)
        acc = jnp.dot(a, w_refs[0][:, cols].astype(BF16), preferred_element_type=F32)
        if epi == "plain":
            out = acc
        elif epi == "silu":
            out = _silu(acc)
        elif epi == "gelu":
            out = 0.5 * acc * (1.0 + lax.erf(acc * (2.0 ** -0.5)))
        elif epi == "glu":
            out = _silu(acc) * jnp.dot(a, w_refs[1][:, cols].astype(BF16), preferred_element_type=F32)
        elif epi == "rope256":
            x1, x2 = acc[:, :LANE], acc[:, LANE:]
            out = jnp.concatenate([x1 * cos - x2 * sin, x1 * sin + x2 * cos], axis=1)
        elif epi == "rope128":
            out = jnp.concatenate(
                [_rope_half_roll(acc[:, g * LANE:(g + 1) * LANE], cos, sin) for g in range(MXU_DIM // LANE)],
                axis=1)
        else:
            raise ValueError(epi)
        o_ref[:, cols] = out.astype(o_ref.dtype)


def _proj_kernel(*refs, epi, tn, col_scale):
    a_ref, w_ref, *table_refs, o_ref = refs
    _proj_chunks(a_ref[...], (w_ref,), table_refs, o_ref, epi=epi, tn=tn, col_scale=col_scale)


def _proj(a, w_stack, layer, *, epi, out_dtype, tables=None, col_scale=1.0, name):
    _, k, n = w_stack.shape
    tm, tn = TM_WIDE, 512
    in_specs = [pl.BlockSpec((tm, k), lambda i, j: (i, 0)),
                pl.BlockSpec((None, k, tn), lambda i, j: (layer, 0, j))]
    args = [a, w_stack]
    if tables is not None:
        in_specs += [pl.BlockSpec((tm, LANE), lambda i, j: (i % (ROWS_B // tm), 0))] * 2
        args += list(tables)
    blocks = (_nbytes((tm, k), BF16) + _nbytes((k, tn), F32) + _nbytes((tm, tn), out_dtype)
              + 2 * _nbytes((tm, LANE), F32))
    temps = _nbytes((k, tn), BF16) + 2 * _nbytes((tm, tn), F32)
    return pl.pallas_call(
        functools.partial(_proj_kernel, epi=epi, tn=tn, col_scale=col_scale),
        grid=(ROWS // tm, n // tn),
        in_specs=in_specs,
        out_specs=pl.BlockSpec((tm, tn), lambda i, j: (i, j)),
        out_shape=jax.ShapeDtypeStruct((ROWS, n), out_dtype),
        compiler_params=pltpu.CompilerParams(
            dimension_semantics=("arbitrary", "arbitrary"),
            vmem_limit_bytes=_vmem_limit(blocks, temps)),
        name=name,
    )(*args)


def _glu_kernel(a_ref, wg_ref, wu_ref, o_ref, *, tn):
    _proj_chunks(a_ref[...], (wg_ref, wu_ref), (), o_ref, epi="glu", tn=tn, col_scale=1.0)


def _glu(a, w_stack, layer):
    k = a.shape[1]
    tm, tn = TM_WIDE, 512
    nt = FFN_HIDDEN // tn
    blocks = _nbytes((tm, k), BF16) + 2 * _nbytes((k, tn), F32) + _nbytes((tm, tn), BF16)
    temps = 2 * _nbytes((k, tn), BF16) + 3 * _nbytes((tm, tn), F32)
    return pl.pallas_call(
        functools.partial(_glu_kernel, tn=tn),
        grid=(ROWS // tm, nt),
        in_specs=[pl.BlockSpec((tm, k), lambda i, j: (i, 0)),
                  pl.BlockSpec((None, k, tn), lambda i, j: (layer, 0, j)),
                  pl.BlockSpec((None, k, tn), lambda i, j: (layer, 0, j + nt))],
        out_specs=pl.BlockSpec((tm, tn), lambda i, j: (i, j)),
        out_shape=jax.ShapeDtypeStruct((ROWS, FFN_HIDDEN), BF16),
        compiler_params=pltpu.CompilerParams(
            dimension_semantics=("arbitrary", "arbitrary"),
            vmem_limit_bytes=_vmem_limit(blocks, temps)),
        name="ffn_glu",
    )(a, w_stack, w_stack)


RN_TM = ROWS_B // 8


def _resid_norm_kernel(a_ref, w_ref, x_ref, gate_ref, gain_ref, shift_ref, scale_ref, xo_ref, h_ref):
    i = pl.program_id(0)
    batch_row = pl.ds(i // (ROWS_B // RN_TM), 1)
    ctx_row = slice(CTX_MOD_ROW, CTX_MOD_ROW + 1)
    latent = _is_latent_rows(i, RN_TM)
    pick = lambda ref, cols: jnp.where(latent, ref[batch_row, cols], ref[ctx_row, cols])
    a = a_ref[...]
    ssq = jnp.zeros((RN_TM, 1), F32)
    for c in range(D_MODEL // MXU_DIM):
        cols = slice(c * MXU_DIM, (c + 1) * MXU_DIM)
        acc = jnp.dot(a, w_ref[:, cols].astype(BF16), preferred_element_type=F32)
        xn = x_ref[:, cols] + pick(gate_ref, cols) * acc
        xo_ref[:, cols] = xn
        ssq = ssq + jnp.sum(xn * xn, axis=-1, keepdims=True)
    inv = lax.rsqrt(ssq * (1.0 / D_MODEL) + EPS)
    for c in range(D_MODEL // MXU_DIM):
        cols = slice(c * MXU_DIM, (c + 1) * MXU_DIM)
        mul = gain_ref[:, cols] * (1.0 + pick(scale_ref, cols))
        h_ref[:, cols] = (xo_ref[:, cols] * inv * mul + pick(shift_ref, cols)).astype(h_ref.dtype)


def _resid_norm(a, w_stack, layer, xs, mods, mod_layer, gate_idx, gain, shift_idx, scale_idx, *, name):
    k = w_stack.shape[1]
    assert k == D_MODEL
    row_spec = lambda: pl.BlockSpec((RN_TM, D_MODEL), lambda i: (i, 0))
    blocks = 2 * _nbytes((RN_TM, D_MODEL), BF16) + 2 * _nbytes((RN_TM, D_MODEL), F32)
    resident = _nbytes((k, D_MODEL), F32)
    temps = 2 * _nbytes((k, MXU_DIM), BF16) + 4 * _nbytes((RN_TM, MXU_DIM), F32)
    return pl.pallas_call(
        _resid_norm_kernel,
        grid=(ROWS // RN_TM,),
        in_specs=[row_spec(),
                  pl.BlockSpec((None, k, D_MODEL), lambda i: (layer, 0, 0), pipeline_mode=pl.Buffered(1)),
                  row_spec(),
                  _mod_spec(mod_layer, gate_idx, D_MODEL),
                  pl.BlockSpec((1, D_MODEL), lambda i: (0, 0)),
                  _mod_spec(mod_layer, shift_idx, D_MODEL), _mod_spec(mod_layer, scale_idx, D_MODEL)],
        out_specs=[row_spec(), row_spec()],
        out_shape=[jax.ShapeDtypeStruct((ROWS, D_MODEL), F32), jax.ShapeDtypeStruct((ROWS, D_MODEL), BF16)],
        compiler_params=pltpu.CompilerParams(
            dimension_semantics=("arbitrary",),
            vmem_limit_bytes=_vmem_limit(blocks, resident + temps)),
        name=name,
    )(a, w_stack, xs, mods, gain.reshape(1, D_MODEL), mods, mods)


def _resid_kernel(a_ref, w_ref, x_ref, g_ref, o_ref, *, tm, tn):
    i = pl.program_id(0)
    batch_row = pl.ds(i // (ROWS_B // tm), 1)
    latent = _is_latent_rows(i, tm)
    a = a_ref[...]
    for c in range(tn // MXU_DIM):
        cols = slice(c * MXU_DIM, (c + 1) * MXU_DIM)
        acc = jnp.dot(a, w_ref[:, cols].astype(BF16), preferred_element_type=F32)
        gate = jnp.where(latent, g_ref[batch_row, cols], g_ref[CTX_MOD_ROW:CTX_MOD_ROW + 1, cols])
        o_ref[:, cols] = x_ref[:, cols] + gate * acc


def _resid(a, w_stack, layer, xs, mods, mod_layer, gate_idx, *, name):
    k = w_stack.shape[1]
    if k <= D_MODEL:
        tm, tn = TM_WIDE, 512
    elif k <= 2 * D_MODEL:
        tm, tn = TM_DEEP, 512
    else:
        tm, tn = TM_DEEP, 256
    blocks = _nbytes((tm, k), BF16) + _nbytes((k, tn), F32) + 2 * _nbytes((tm, tn), F32)
    temps = _nbytes((k, tn), BF16) + 2 * _nbytes((tm, tn), F32)
    return pl.pallas_call(
        functools.partial(_resid_kernel, tm=tm, tn=tn),
        grid=(ROWS // tm, D_MODEL // tn),
        in_specs=[pl.BlockSpec((tm, k), lambda i, j: (i, 0)),
                  pl.BlockSpec((None, k, tn), lambda i, j: (layer, 0, j)),
                  pl.BlockSpec((tm, tn), lambda i, j: (i, j)),
                  _mod_spec(mod_layer, gate_idx, tn, lambda i, j: j)],
        out_specs=pl.BlockSpec((tm, tn), lambda i, j: (i, j)),
        out_shape=jax.ShapeDtypeStruct((ROWS, D_MODEL), F32),
        compiler_params=pltpu.CompilerParams(
            dimension_semantics=("arbitrary", "arbitrary"),
            vmem_limit_bytes=_vmem_limit(blocks, temps)),
        name=name,
    )(a, w_stack, xs, mods)


def _attn_scores(q_ref, k_ref, s_ref, m_ref, *, keys, rows_a, tq):
    for m in range(2):
        cols = slice(m * DA_HEAD_DIM, (m + 1) * DA_HEAD_DIM)
        qm = q_ref[:, cols]
        mx = None
        for lo in range(keys[0], keys[1], rows_a):
            rows = slice(lo, lo + rows_a)
            s = lax.dot_general(k_ref[rows, cols], qm, (((1,), (1,)), ((), ())), preferred_element_type=F32)
            s_ref[m, rows, :] = s
            part = jnp.max(s.reshape(rows_a // SUBLANE, SUBLANE, tq), axis=0)
            mx = part if mx is None else jnp.maximum(mx, part)
        m_ref[m] = jnp.broadcast_to(jnp.max(mx, axis=0, keepdims=True), (SUBLANE, tq))


def _attn_finish(lam, vt_ref, g_ref, s_ref, m_ref, o_ref, *, keys, tq, out_scale):
    heads = []
    for m in range(2):
        mrow = m_ref[m][0:1, :]
        acc = jnp.zeros((DA_HEAD_W, tq), F32)
        lsum = jnp.zeros((SUBLANE, tq), F32)
        for lo in range(keys[0], keys[1], MXU_DIM):
            rows = slice(lo, lo + MXU_DIM)
            p = jnp.exp2(s_ref[m, rows, :] - mrow)
            lsum = lsum + jnp.sum(p.reshape(MXU_DIM // SUBLANE, SUBLANE, tq), axis=0)
            acc = acc + jnp.dot(vt_ref[:, rows], p.astype(BF16), preferred_element_type=F32)
        heads.append(acc * (1.0 / jnp.sum(lsum, axis=0, keepdims=True)))
    o = heads[0] - lam * heads[1]
    o = o * lax.rsqrt(jnp.mean(o * o, axis=0, keepdims=True) + EPS) * g_ref[...] * out_scale
    o_ref[...] = o.T.astype(o_ref.dtype)


DA_TQ = CTX_LEN
DA_LATENT_TILES = SEQ // DA_TQ
assert DA_LATENT_TILES % 2 == 0


def _diff_attn_kernel(lam_ref, q_ref, k_ref, v_ref, g_ref, o_ref, vt_ref, sa_ref, sb_ref, ma_ref, mb_ref,
                      *, out_scale):
    t = pl.program_id(2)
    lam = lam_ref[0]
    all_keys, ctx_keys = (0, ROWS_B), (SEQ, ROWS_B)
    scores = functools.partial(_attn_scores, q_ref, k_ref, tq=DA_TQ)
    finish = functools.partial(_attn_finish, lam, vt_ref, g_ref, tq=DA_TQ, out_scale=out_scale)

    @pl.when(t == 0)
    def _():
        for lo in range(0, ROWS_B, MXU_DIM):
            vt_ref[:, lo:lo + MXU_DIM] = v_ref[lo:lo + MXU_DIM, :].astype(F32).T.astype(BF16)
        scores(sa_ref, ma_ref, keys=all_keys, rows_a=TM_DEEP)

    @pl.when(jnp.logical_and(jnp.logical_and(t > 0, t < DA_LATENT_TILES), t % 2 == 0))
    def _():
        scores(sa_ref, ma_ref, keys=all_keys, rows_a=TM_DEEP)
        finish(sb_ref, mb_ref, o_ref, keys=all_keys)

    @pl.when(jnp.logical_and(t < DA_LATENT_TILES, t % 2 == 1))
    def _():
        scores(sb_ref, mb_ref, keys=all_keys, rows_a=TM_DEEP)
        finish(sa_ref, ma_ref, o_ref, keys=all_keys)

    @pl.when(t == DA_LATENT_TILES)
    def _():
        scores(sa_ref, ma_ref, keys=ctx_keys, rows_a=CTX_LEN)
        finish(sb_ref, mb_ref, o_ref, keys=all_keys)

    @pl.when(t == DA_LATENT_TILES + 1)
    def _():
        finish(sa_ref, ma_ref, o_ref, keys=ctx_keys)


def _diff_attn(qkv, lam, subln_g, out_scale):
    tiles_b = ROWS_B // DA_TQ
    kcol0 = DA_WIDTH // DA_HEAD_W
    vcol0 = 2 * kcol0
    blocks = 2 * _nbytes((DA_TQ, DA_HEAD_W), BF16) + 2 * _nbytes((ROWS_B, DA_HEAD_W), BF16)
    scratch = (_nbytes((DA_HEAD_W, ROWS_B), BF16) + 4 * _nbytes((ROWS_B, DA_TQ), F32)
               + 4 * _nbytes((SUBLANE, DA_TQ), F32))
    return pl.pallas_call(
        functools.partial(_diff_attn_kernel, out_scale=out_scale),
        grid=(BATCH, DA_HEADS, DA_LATENT_TILES + 2),
        in_specs=[
            pl.BlockSpec(memory_space=pltpu.SMEM),
            pl.BlockSpec((DA_TQ, DA_HEAD_W),
                         lambda b, h, t: (b * tiles_b + jnp.minimum(t, DA_LATENT_TILES), h)),
            pl.BlockSpec((ROWS_B, DA_HEAD_W), lambda b, h, t: (b, kcol0 + h)),
            pl.BlockSpec((ROWS_B, DA_HEAD_W), lambda b, h, t: (b, vcol0 + h)),
            pl.BlockSpec((DA_HEAD_W, 1), lambda b, h, t: (0, 0)),
        ],
        out_specs=pl.BlockSpec((DA_TQ, DA_HEAD_W),
                               lambda b, h, t: (b * tiles_b + jnp.maximum(t - 1, 0), h)),
        out_shape=jax.ShapeDtypeStruct((ROWS, DA_WIDTH), BF16),
        scratch_shapes=[pltpu.VMEM((DA_HEAD_W, ROWS_B), BF16),
                        pltpu.VMEM((2, ROWS_B, DA_TQ), F32), pltpu.VMEM((2, ROWS_B, DA_TQ), F32),
                        pltpu.VMEM((2, SUBLANE, DA_TQ), F32), pltpu.VMEM((2, SUBLANE, DA_TQ), F32)],
        compiler_params=pltpu.CompilerParams(
            dimension_semantics=("arbitrary", "arbitrary", "arbitrary"),
            vmem_limit_bytes=_vmem_limit(blocks, scratch + 4 * _nbytes((TM_DEEP, DA_TQ), F32))),
        name="diff_attn",
    )(lam, qkv, qkv, qkv, subln_g.reshape(DA_HEAD_W, 1))


def _sg_gate_kernel(z_ref, vg_ref, ws_ref, bs_ref, o_ref, *, chunks):
    v = z_ref[:, SG_WIDTH:]
    v = v * lax.rsqrt(jnp.mean(v * v, axis=-1, keepdims=True) + EPS) * vg_ref[...]
    for c in range(chunks):
        r0 = c * SG_CHUNK
        for g in range(SG_GROUPS):
            c0 = g * SG_GROUP_DIM
            vg = v[r0:r0 + SG_CHUNK, c0:c0 + SG_GROUP_DIM].astype(BF16)
            mixed = jnp.dot(ws_ref[g].astype(BF16), vg, preferred_element_type=F32) + bs_ref[g]
            u = z_ref[r0:r0 + SG_CHUNK, c0:c0 + SG_GROUP_DIM]
            o_ref[r0:r0 + SG_CHUNK, c0:c0 + SG_GROUP_DIM] = (u * mixed).astype(o_ref.dtype)


def _sg_gate(z, v_gain, w_s, b_s):
    chunks = 2
    tm = chunks * SG_CHUNK
    bs_b = jnp.broadcast_to(b_s[:, :, None], (SG_GROUPS, SG_CHUNK, SG_GROUP_DIM))
    blocks = (_nbytes((tm, 2 * SG_WIDTH), F32) + _nbytes((tm, SG_WIDTH), BF16)
              + 2 * _nbytes((SG_GROUPS, SG_CHUNK, SG_CHUNK), F32))
    return pl.pallas_call(
        functools.partial(_sg_gate_kernel, chunks=chunks),
        grid=(ROWS // tm,),
        in_specs=[pl.BlockSpec((tm, 2 * SG_WIDTH), lambda i: (i, 0)),
                  pl.BlockSpec((1, SG_WIDTH), lambda i: (0, 0)),
                  pl.BlockSpec((SG_GROUPS, SG_CHUNK, SG_CHUNK), lambda i: (0, 0, 0)),
                  pl.BlockSpec((SG_GROUPS, SG_CHUNK, SG_GROUP_DIM), lambda i: (0, 0, 0))],
        out_specs=pl.BlockSpec((tm, SG_WIDTH), lambda i: (i, 0)),
        out_shape=jax.ShapeDtypeStruct((ROWS, SG_WIDTH), BF16),
        compiler_params=pltpu.CompilerParams(
            dimension_semantics=("arbitrary",),
            vmem_limit_bytes=_vmem_limit(blocks, 4 * _nbytes((tm, SG_WIDTH), F32))),
        name="sg_gate",
    )(z, v_gain.reshape(1, SG_WIDTH), w_s, bs_b)


def _ret_kernel(*refs, backward):
    if backward:
        g_ref, q_ref, k_ref, v_ref, dec_ref, xi_ref, zeta_ref, of_ref, gp_ref, o_ref, s_ref = refs
    else:
        g_ref, q_ref, k_ref, v_ref, dec_ref, xi_ref, zeta_ref, o_ref, s_ref = refs

    @pl.when(pl.program_id(1) == 0)
    def _():
        s_ref[...] = jnp.zeros_like(s_ref)

    for h in range(RET_HEADS):
        kc = slice(h * RET_KEY_DIM, (h + 1) * RET_KEY_DIM)
        vc = slice(h * RET_VAL_DIM, (h + 1) * RET_VAL_DIM)
        qh = q_ref[:, kc]
        kh = k_ref[:, kc]
        vh = v_ref[:, vc]
        scores = lax.dot_general(qh, kh, (((1,), (1,)), ((), ())), preferred_element_type=F32) * dec_ref[h]
        state = s_ref[h]
        o = jnp.dot(scores.astype(BF16), vh, preferred_element_type=F32)
        cross = jnp.dot(qh, state.astype(BF16), preferred_element_type=F32)
        xi = xi_ref[h]
        o = o + jnp.concatenate(
            [cross[:, e * LANE:(e + 1) * LANE] * xi for e in range(RET_VAL_DIM // LANE)], axis=1)
        zeta = zeta_ref[h]
        kz = jnp.concatenate(
            [kh[:, e * LANE:(e + 1) * LANE].astype(F32) * zeta for e in range(RET_KEY_DIM // LANE)], axis=1)
        kz_t = kz.T.astype(BF16)
        s_ref[h] = g_ref[h] * state + jnp.dot(kz_t, vh, preferred_element_type=F32)
        if backward:
            o = o + of_ref[:, vc]
            o = o * lax.rsqrt(jnp.mean(o * o, axis=-1, keepdims=True) + EPS)
            o_ref[:, vc] = (gp_ref[:, vc] * o).astype(o_ref.dtype)
        else:
            o_ref[:, vc] = o


def _ret_consts(log_gamma, backward):
    pos = jnp.arange(RET_CHUNK, dtype=F32)
    dist = pos[:, None] - pos[None, :]
    if backward:
        dist = -dist
    lg = log_gamma[:, None, None]
    decay = jnp.where(dist >= 0, jnp.exp(lg * jnp.maximum(dist, 0.0)), 0.0)
    pos_in_scan = (RET_CHUNK - 1.0 - pos) if backward else pos
    xi = jnp.exp(log_gamma[:, None] * (pos_in_scan + 1.0))
    zeta = jnp.exp(log_gamma[:, None] * (RET_CHUNK - 1.0 - pos_in_scan))
    g_chunk = jnp.exp(log_gamma * RET_CHUNK)
    bcast = lambda t: jnp.broadcast_to(t[:, :, None], (RET_HEADS, RET_CHUNK, LANE))
    return g_chunk, decay, bcast(xi), bcast(zeta)


def _ret_scan(q, k, v, log_gamma, *, backward, o_fwd=None, gproj=None):
    n_chunks = ROWS_B // RET_CHUNK
    x_chunks = SEQ // RET_CHUNK
    g_chunk, decay, xi, zeta = _ret_consts(log_gamma, backward)
    if backward:
        chunk_of = lambda b, t: (b * n_chunks + (n_chunks - 1 - t), 0)
    else:
        chunk_of = lambda b, t: (b * n_chunks + (t + x_chunks) % n_chunks, 0)
    qk_w = RET_HEADS * RET_KEY_DIM
    v_w = RET_HEADS * RET_VAL_DIM
    const_spec = lambda w: pl.BlockSpec((RET_HEADS, RET_CHUNK, w), lambda b, t: (0, 0, 0))
    in_specs = [pl.BlockSpec(memory_space=pltpu.SMEM),
                pl.BlockSpec((RET_CHUNK, qk_w), chunk_of),
                pl.BlockSpec((RET_CHUNK, qk_w), chunk_of),
                pl.BlockSpec((RET_CHUNK, v_w), chunk_of),
                const_spec(RET_CHUNK), const_spec(LANE), const_spec(LANE)]
    args = [g_chunk, q, k, v, decay, xi, zeta]
    blocks = (2 * _nbytes((RET_CHUNK, qk_w), BF16) + _nbytes((RET_CHUNK, v_w), BF16)
              + _nbytes((RET_HEADS, RET_CHUNK, RET_CHUNK + 2 * LANE), F32)
              + _nbytes((RET_CHUNK, v_w), F32))
    if backward:
        in_specs += [pl.BlockSpec((RET_CHUNK, v_w), chunk_of), pl.BlockSpec((RET_CHUNK, v_w), chunk_of)]
        args += [o_fwd, gproj]
        blocks += 2 * _nbytes((RET_CHUNK, v_w), F32)
        out_dtype = BF16
    else:
        out_dtype = F32
    state_bytes = _nbytes((RET_HEADS, RET_KEY_DIM, RET_VAL_DIM), F32)
    return pl.pallas_call(
        functools.partial(_ret_kernel, backward=backward),
        grid=(BATCH, n_chunks),
        in_specs=in_specs,
        out_specs=pl.BlockSpec((RET_CHUNK, v_w), chunk_of),
        out_shape=jax.ShapeDtypeStruct((ROWS, v_w), out_dtype),
        scratch_shapes=[pltpu.VMEM((RET_HEADS, RET_KEY_DIM, RET_VAL_DIM), F32)],
        compiler_params=pltpu.CompilerParams(
            dimension_semantics=("arbitrary", "arbitrary"),
            vmem_limit_bytes=_vmem_limit(blocks, state_bytes + (8 << 20))),
        name="ret_bwd" if backward else "ret_fwd",
    )(*args)


def _final_norm_kernel(x_ref, g_ref, o_ref):
    x = x_ref[...]
    o_ref[...] = x * lax.rsqrt(jnp.mean(x * x, axis=-1, keepdims=True) + EPS) * g_ref[...]


def _final_norm(xs, gain):
    blocks = 2 * _nbytes((NORM_TM, D_MODEL), F32)
    return pl.pallas_call(
        _final_norm_kernel,
        grid=(BATCH, SEQ // NORM_TM),
        in_specs=[pl.BlockSpec((NORM_TM, D_MODEL), lambda b, t: (b * (ROWS_B // NORM_TM) + t, 0)),
                  pl.BlockSpec((1, D_MODEL), lambda b, t: (0, 0))],
        out_specs=pl.BlockSpec((None, NORM_TM, D_MODEL), lambda b, t: (b, t, 0)),
        out_shape=jax.ShapeDtypeStruct((BATCH, SEQ, D_MODEL), F32),
        compiler_params=pltpu.CompilerParams(
            dimension_semantics=("arbitrary", "arbitrary"),
            vmem_limit_bytes=_vmem_limit(blocks, 2 * _nbytes((NORM_TM, D_MODEL), F32))),
        name="final_norm",
    )(xs, gain.reshape(1, D_MODEL))


def _rope_tables(head_dim):
    rows = SEQ // GRID_W
    row = jnp.broadcast_to(jnp.arange(rows)[:, None], (rows, GRID_W)).reshape(-1).astype(F32)
    col = jnp.broadcast_to(jnp.arange(GRID_W)[None, :], (rows, GRID_W)).reshape(-1).astype(F32)
    n_freq = head_dim // 4
    inv_freq = ROPE_BASE ** (-jnp.arange(n_freq, dtype=F32) / n_freq)
    ang = jnp.concatenate([row[:, None] * inv_freq, col[:, None] * inv_freq], axis=-1)
    cos = jnp.concatenate([jnp.cos(ang), jnp.ones((CTX_LEN, head_dim // 2), F32)], axis=0)
    sin = jnp.concatenate([jnp.sin(ang), jnp.zeros((CTX_LEN, head_dim // 2), F32)], axis=0)
    return cos, sin


def kernel(x, c, ctx, c_ctx, ada_w, ada_b, norm_mix_g, norm_ffn_g, ffn_w_gate_up, ffn_w_down,
           da_w_qkv, da_w_o, da_lambda, da_subln_g,
           sg_w_in, sg_v_g, sg_w_s, sg_b_s, sg_w_out,
           ret_w_q, ret_w_k, ret_w_v, ret_w_g, ret_w_o, ret_decay, final_norm_g):
    xs = jnp.concatenate([x, ctx], axis=1).reshape(ROWS, D_MODEL)
    cvec = jnp.concatenate([c, c_ctx[None, :], jnp.zeros((MOD_ROWS - BATCH - 1, D_MODEL), F32)], axis=0)
    mods = _ada_tables(cvec, ada_w, ada_b)

    da_cos, da_sin = _rope_tables(DA_HEAD_DIM)
    da_tables = (jnp.concatenate([da_cos, da_cos], axis=1), jnp.concatenate([-da_sin, da_sin], axis=1))
    ret_tables = _rope_tables(RET_KEY_DIM)
    da_q_scale = DA_HEAD_DIM ** -0.5 * math.log2(math.e)

    for i in range(DEPTH):
        kind = i % N_MIXERS
        j = i // N_MIXERS
        h = _normmod(xs, norm_mix_g[i], mods, i, 0, 1)
        ffn_norm = (norm_ffn_g[i], 3, 4)
        if kind == 0:
            lambda_init = 0.8 - 0.6 * math.exp(-0.3 * i)
            lv = da_lambda[j].astype(F32)
            lam = (jnp.exp(jnp.sum(lv[0] * lv[1])) - jnp.exp(jnp.sum(lv[2] * lv[3])) + lambda_init).reshape(1)
            qkv = _proj(h, da_w_qkv, j, epi="rope128", out_dtype=BF16, tables=da_tables,
                        col_scale=da_q_scale, name="da_qkv")
            y = _diff_attn(qkv, lam, da_subln_g[j], 1.0 - lambda_init)
            xs, h = _resid_norm(y, da_w_o, j, xs, mods, i, 2, *ffn_norm, name="da_out")
        elif kind == 1:
            z = _proj(h, sg_w_in, j, epi="gelu", out_dtype=F32, name="sg_in")
            y = _sg_gate(z, sg_v_g[j], sg_w_s[j], sg_b_s[j])
            xs, h = _resid_norm(y, sg_w_out, j, xs, mods, i, 2, *ffn_norm, name="sg_out")
        else:
            log_gamma = -jnp.exp(ret_decay[j].astype(F32))
            q = _proj(h, ret_w_q, j, epi="rope256", out_dtype=BF16, tables=ret_tables, name="ret_q")
            k = _proj(h, ret_w_k, j, epi="rope256", out_dtype=BF16, tables=ret_tables,
                      col_scale=RET_KEY_DIM ** -0.5, name="ret_k")
            v = _proj(h, ret_w_v, j, epi="plain", out_dtype=BF16, name="ret_v")
            gp = _proj(h, ret_w_g, j, epi="silu", out_dtype=F32, name="ret_g")
            o_f = _ret_scan(q, k, v, log_gamma[0], backward=False)
            y = _ret_scan(q, k, v, log_gamma[1], backward=True, o_fwd=o_f, gproj=gp)
            xs = _resid(y, ret_w_o, j, xs, mods, i, 2, name="ret_out")
            h = _normmod(xs, norm_ffn_g[i], mods, i, 3, 4)
        act = _glu(h, ffn_w_gate_up, i)
        xs = _resid(act, ffn_w_down, i, xs, mods, i, 5, name="ffn_down")
    return _final_norm(xs, final_norm_g)
```

```python
import functools
import math

import jax
import jax.numpy as jnp
from jax import lax
from jax.experimental import pallas as pl
from jax.experimental.pallas import tpu as pltpu

D_MODEL = 2048
BATCH = 2
SEQ = 4096
DEPTH = 4
GRID_W = 64
CTX_LEN = 256
N_MIXERS = 3
EPS = 1e-6
ROPE_BASE = 10000.0
N_MOD = 6

DA_HEAD_DIM = 128
DA_HEADS = D_MODEL // (2 * DA_HEAD_DIM)
DA_WIDTH = 2 * DA_HEADS * DA_HEAD_DIM
DA_HEAD_W = 2 * DA_HEAD_DIM

SG_CHUNK = 128
SG_GROUP_DIM = 128
SG_WIDTH = D_MODEL
SG_GROUPS = SG_WIDTH // SG_GROUP_DIM

RET_HEADS = D_MODEL // 256
RET_KEY_DIM = D_MODEL // RET_HEADS
RET_VAL_DIM = 2 * D_MODEL // RET_HEADS
RET_CHUNK = 256

FFN_HIDDEN = -((-8 * D_MODEL) // (3 * 256)) * 256

ROWS_B = SEQ + CTX_LEN
ROWS = BATCH * ROWS_B
TM_WIDE = ROWS_B // 2
TM_DEEP = ROWS_B // 4
NORM_TM = CTX_LEN
NORM_STRIP = 16
MOD_ROWS = 8
CTX_MOD_ROW = BATCH

LANE = 128
SUBLANE = 8
MXU_DIM = 256
VMEM_LIMIT_CAP = 60 * 1024 * 1024
VMEM_LIMIT_FLOOR = 40 * 1024 * 1024

BF16 = jnp.bfloat16
F32 = jnp.float32


def _vmem_limit(block_bytes, temp_bytes):
    return int(max(VMEM_LIMIT_FLOOR, min(VMEM_LIMIT_CAP, 2 * block_bytes + temp_bytes + (4 << 20))))


def _nbytes(shape, dtype):
    return math.prod(shape) * jnp.dtype(dtype).itemsize


def _is_latent_rows(tile_idx, tm):
    row = lax.broadcasted_iota(jnp.int32, (tm, 1), 0) + (tile_idx % (ROWS_B // tm)) * tm
    return row < SEQ


def _silu(x):
    return x * (1.0 / (1.0 + jnp.exp(-x)))


def _ada_kernel(c_ref, w_ref, b_ref, o_ref):
    s = _silu(c_ref[...]).astype(BF16)
    acc = jnp.dot(s, w_ref[...].astype(BF16), preferred_element_type=F32)
    o_ref[...] = acc + b_ref[...]


def _ada_tables(cvec, ada_w, ada_b):
    tn = 1024
    n = N_MOD * D_MODEL
    blocks = _nbytes((D_MODEL, tn), F32) + _nbytes((MOD_ROWS, tn), F32) * 2
    return pl.pallas_call(
        _ada_kernel,
        grid=(DEPTH, n // tn),
        in_specs=[
            pl.BlockSpec((MOD_ROWS, D_MODEL), lambda l, j: (0, 0)),
            pl.BlockSpec((None, D_MODEL, tn), lambda l, j: (l, 0, j)),
            pl.BlockSpec((None, 1, tn), lambda l, j: (l, 0, j)),
        ],
        out_specs=pl.BlockSpec((None, MOD_ROWS, tn), lambda l, j: (l, 0, j)),
        out_shape=jax.ShapeDtypeStruct((DEPTH, MOD_ROWS, n), F32),
        compiler_params=pltpu.CompilerParams(
            dimension_semantics=("arbitrary", "arbitrary"),
            vmem_limit_bytes=_vmem_limit(blocks, _nbytes((D_MODEL, tn), BF16))),
        name="ada_tables",
    )(cvec, ada_w, ada_b.reshape(DEPTH, 1, n))


def _mod_spec(layer, which, width, col_of=lambda *idx: 0):
    blocks_per_vec = D_MODEL // width
    return pl.BlockSpec((None, MOD_ROWS, width), lambda *idx: (layer, 0, which * blocks_per_vec + col_of(*idx)))


def _normmod_kernel(x_ref, g_ref, shift_ref, scale_ref, o_ref):
    i = pl.program_id(0)
    tiles_b = ROWS_B // NORM_TM
    row = pl.ds(jnp.where(i % tiles_b == tiles_b - 1, CTX_MOD_ROW, i // tiles_b), 1)
    mul = g_ref[...] * (1.0 + scale_ref[row, :])
    add = shift_ref[row, :]
    for r in range(NORM_TM // NORM_STRIP):
        rows = slice(r * NORM_STRIP, (r + 1) * NORM_STRIP)
        x = x_ref[rows, :]
        inv = lax.rsqrt(jnp.mean(x * x, axis=-1, keepdims=True) + EPS)
        o_ref[rows, :] = (x * inv * mul + add).astype(o_ref.dtype)


def _normmod(xs, gain, mods, layer, shift_idx, scale_idx):
    blocks = _nbytes((NORM_TM, D_MODEL), F32) + _nbytes((NORM_TM, D_MODEL), BF16)
    return pl.pallas_call(
        _normmod_kernel,
        grid=(ROWS // NORM_TM,),
        in_specs=[pl.BlockSpec((NORM_TM, D_MODEL), lambda i: (i, 0)),
                  pl.BlockSpec((1, D_MODEL), lambda i: (0, 0)),
                  _mod_spec(layer, shift_idx, D_MODEL), _mod_spec(layer, scale_idx, D_MODEL)],
        out_specs=pl.BlockSpec((NORM_TM, D_MODEL), lambda i: (i, 0)),
        out_shape=jax.ShapeDtypeStruct((ROWS, D_MODEL), BF16),
        compiler_params=pltpu.CompilerParams(
            dimension_semantics=("arbitrary",),
            vmem_limit_bytes=_vmem_limit(blocks, 0)),
        name="normmod",
    )(xs, gain.reshape(1, D_MODEL), mods, mods)


def _assemble_norm_kernel(x_ref, ctx_ref, g_ref, shift_ref, scale_ref, xs_ref, h_ref):
    b, t = pl.program_id(0), pl.program_id(1)
    is_ctx = t == SEQ // NORM_TM
    row = pl.ds(jnp.where(is_ctx, CTX_MOD_ROW, b), 1)
    mul = g_ref[...] * (1.0 + scale_ref[row, :])
    add = shift_ref[row, :]

    def emit(src_ref):
        for r in range(NORM_TM // NORM_STRIP):
            rows = slice(r * NORM_STRIP, (r + 1) * NORM_STRIP)
            x = src_ref[rows, :]
            xs_ref[rows, :] = x
            inv = lax.rsqrt(jnp.mean(x * x, axis=-1, keepdims=True) + EPS)
            h_ref[rows, :] = (x * inv * mul + add).astype(h_ref.dtype)

    @pl.when(jnp.logical_not(is_ctx))
    def _():
        emit(x_ref)

    @pl.when(is_ctx)
    def _():
        emit(ctx_ref)


def _assemble_norm(x, ctx, gain, mods, layer, shift_idx, scale_idx):
    tiles_b = ROWS_B // NORM_TM
    x_tiles = SEQ // NORM_TM
    out_spec = lambda: pl.BlockSpec((NORM_TM, D_MODEL), lambda b, t: (b * tiles_b + t, 0))
    blocks = 3 * _nbytes((NORM_TM, D_MODEL), F32) + _nbytes((NORM_TM, D_MODEL), BF16)
    return pl.pallas_call(
        _assemble_norm_kernel,
        grid=(BATCH, tiles_b),
        in_specs=[pl.BlockSpec((None, NORM_TM, D_MODEL), lambda b, t: (b, jnp.minimum(t, x_tiles - 1), 0)),
                  pl.BlockSpec((None, CTX_LEN, D_MODEL), lambda b, t: (b, 0, 0)),
                  pl.BlockSpec((1, D_MODEL), lambda b, t: (0, 0)),
                  _mod_spec(layer, shift_idx, D_MODEL), _mod_spec(layer, scale_idx, D_MODEL)],
        out_specs=[out_spec(), out_spec()],
        out_shape=[jax.ShapeDtypeStruct((ROWS, D_MODEL), F32), jax.ShapeDtypeStruct((ROWS, D_MODEL), BF16)],
        compiler_params=pltpu.CompilerParams(
            dimension_semantics=("arbitrary", "arbitrary"),
            vmem_limit_bytes=_vmem_limit(blocks, 0)),
        name="assemble_norm",
    )(x, ctx, gain.reshape(1, D_MODEL), mods, mods)


def _rope_half_roll(acc, cos2, sin2):
    return acc * cos2 + pltpu.roll(acc, DA_HEAD_DIM // 2, 1) * sin2


def _proj_chunks(a, w_refs, table_refs, o_ref, *, epi, tn, col_scale):
    if epi == "rope256":
        cos, sin = table_refs[0][...] * col_scale, table_refs[1][...] * col_scale
    elif epi == "rope128":
        j = pl.program_id(1)
        tiles_per_part = DA_WIDTH // tn
        rotated = j < 2 * tiles_per_part
        scale = jnp.where(j < tiles_per_part, col_scale, 1.0)
        cos = jnp.where(rotated, table_refs[0][...] * scale, 1.0)
        sin = jnp.where(rotated, table_refs[1][...] * scale, 0.0)
    for c in range(tn // MXU_DIM):
        cols = slice(c * MXU_DIM, (c + 1) * MXU_DIM)
        acc = jnp.dot(a, w_refs[0][:, cols].astype(BF16), preferred_element_type=F32)
        if epi == "plain":
            out = acc
        elif epi == "silu":
            out = _silu(acc)
        elif epi == "gelu":
            out = 0.5 * acc * (1.0 + lax.erf(acc * (2.0 ** -0.5)))
        elif epi == "glu":
            out = _silu(acc) * jnp.dot(a, w_refs[1][:, cols].astype(BF16), preferred_element_type=F32)
        elif epi == "rope256":
            x1, x2 = acc[:, :LANE], acc[:, LANE:]
            out = jnp.concatenate([x1 * cos - x2 * sin, x1 * sin + x2 * cos], axis=1)
        elif epi == "rope128":
            out = jnp.concatenate(
                [_rope_half_roll(acc[:, g * LANE:(g + 1) * LANE], cos, sin) for g in range(MXU_DIM // LANE)],
                axis=1)
        else:
            raise ValueError(epi)
        o_ref[:, cols] = out.astype(o_ref.dtype)


def _proj_kernel(*refs, epi, tn, col_scale):
    a_ref, w_ref, *table_refs, o_ref = refs
    _proj_chunks(a_ref[...], (w_ref,), table_refs, o_ref, epi=epi, tn=tn, col_scale=col_scale)


def _proj(a, w_stack, layer, *, epi, out_dtype, tables=None, col_scale=1.0, name):
    _, k, n = w_stack.shape
    tm = TM_WIDE
    tn = 1024 if jnp.dtype(out_dtype).itemsize == 2 else 512
    in_specs = [pl.BlockSpec((tm, k), lambda i, j: (i, 0)),
                pl.BlockSpec((None, k, tn), lambda i, j: (layer, 0, j))]
    args = [a, w_stack]
    if tables is not None:
        in_specs += [pl.BlockSpec((tm, LANE), lambda i, j: (i % (ROWS_B // tm), 0))] * 2
        args += list(tables)
    blocks = (_nbytes((tm, k), BF16) + _nbytes((k, tn), F32) + _nbytes((tm, tn), out_dtype)
              + 2 * _nbytes((tm, LANE), F32))
    temps = _nbytes((k, tn), BF16) + 2 * _nbytes((tm, tn), F32)
    return pl.pallas_call(
        functools.partial(_proj_kernel, epi=epi, tn=tn, col_scale=col_scale),
        grid=(ROWS // tm, n // tn),
        in_specs=in_specs,
        out_specs=pl.BlockSpec((tm, tn), lambda i, j: (i, j)),
        out_shape=jax.ShapeDtypeStruct((ROWS, n), out_dtype),
        compiler_params=pltpu.CompilerParams(
            dimension_semantics=("arbitrary", "arbitrary"),
            vmem_limit_bytes=_vmem_limit(blocks, temps)),
        name=name,
    )(*args)


def _glu_kernel(a_ref, wg_ref, wu_ref, o_ref, *, tn):
    _proj_chunks(a_ref[...], (wg_ref, wu_ref), (), o_ref, epi="glu", tn=tn, col_scale=1.0)


def _glu(a, w_stack, layer):
    k = a.shape[1]
    tm, tn = TM_WIDE, 512
    nt = FFN_HIDDEN // tn
    blocks = _nbytes((tm, k), BF16) + 2 * _nbytes((k, tn), F32) + _nbytes((tm, tn), BF16)
    temps = 2 * _nbytes((k, tn), BF16) + 3 * _nbytes((tm, tn), F32)
    return pl.pallas_call(
        functools.partial(_glu_kernel, tn=tn),
        grid=(ROWS // tm, nt),
        in_specs=[pl.BlockSpec((tm, k), lambda i, j: (i, 0)),
                  pl.BlockSpec((None, k, tn), lambda i, j: (layer, 0, j)),
                  pl.BlockSpec((None, k, tn), lambda i, j: (layer, 0, j + nt))],
        out_specs=pl.BlockSpec((tm, tn), lambda i, j: (i, j)),
        out_shape=jax.ShapeDtypeStruct((ROWS, FFN_HIDDEN), BF16),
        compiler_params=pltpu.CompilerParams(
            dimension_semantics=("arbitrary", "arbitrary"),
            vmem_limit_bytes=_vmem_limit(blocks, temps)),
        name="ffn_glu",
    )(a, w_stack, w_stack)


RN_TM = ROWS_B // 8


def _resid_norm_kernel(a_ref, w_ref, x_ref, gate_ref, gain_ref, shift_ref, scale_ref, xo_ref, h_ref):
    i = pl.program_id(0)
    batch_row = pl.ds(i // (ROWS_B // RN_TM), 1)
    ctx_row = slice(CTX_MOD_ROW, CTX_MOD_ROW + 1)
    latent = _is_latent_rows(i, RN_TM)
    pick = lambda ref, cols: jnp.where(latent, ref[batch_row, cols], ref[ctx_row, cols])
    a = a_ref[...]
    ssq = jnp.zeros((RN_TM, 1), F32)
    for c in range(D_MODEL // MXU_DIM):
        cols = slice(c * MXU_DIM, (c + 1) * MXU_DIM)
        acc = jnp.dot(a, w_ref[:, cols].astype(BF16), preferred_element_type=F32)
        xn = x_ref[:, cols] + pick(gate_ref, cols) * acc
        xo_ref[:, cols] = xn
        ssq = ssq + jnp.sum(xn * xn, axis=-1, keepdims=True)
    inv = lax.rsqrt(ssq * (1.0 / D_MODEL) + EPS)
    for c in range(D_MODEL // MXU_DIM):
        cols = slice(c * MXU_DIM, (c + 1) * MXU_DIM)
        mul = gain_ref[:, cols] * (1.0 + pick(scale_ref, cols))
        h_ref[:, cols] = (xo_ref[:, cols] * inv * mul + pick(shift_ref, cols)).astype(h_ref.dtype)


def _resid_norm(a, w_stack, layer, xs, mods, mod_layer, gate_idx, gain, shift_idx, scale_idx, *, name):
    k = w_stack.shape[1]
    assert k == D_MODEL
    row_spec = lambda: pl.BlockSpec((RN_TM, D_MODEL), lambda i: (i, 0))
    blocks = 2 * _nbytes((RN_TM, D_MODEL), BF16) + 2 * _nbytes((RN_TM, D_MODEL), F32)
    resident = _nbytes((k, D_MODEL), F32)
    temps = 2 * _nbytes((k, MXU_DIM), BF16) + 4 * _nbytes((RN_TM, MXU_DIM), F32)
    return pl.pallas_call(
        _resid_norm_kernel,
        grid=(ROWS // RN_TM,),
        in_specs=[row_spec(),
                  pl.BlockSpec((None, k, D_MODEL), lambda i: (layer, 0, 0), pipeline_mode=pl.Buffered(1)),
                  row_spec(),
                  _mod_spec(mod_layer, gate_idx, D_MODEL),
                  pl.BlockSpec((1, D_MODEL), lambda i: (0, 0)),
                  _mod_spec(mod_layer, shift_idx, D_MODEL), _mod_spec(mod_layer, scale_idx, D_MODEL)],
        out_specs=[row_spec(), row_spec()],
        out_shape=[jax.ShapeDtypeStruct((ROWS, D_MODEL), F32), jax.ShapeDtypeStruct((ROWS, D_MODEL), BF16)],
        compiler_params=pltpu.CompilerParams(
            dimension_semantics=("arbitrary",),
            vmem_limit_bytes=_vmem_limit(blocks, resident + temps)),
        name=name,
    )(a, w_stack, xs, mods, gain.reshape(1, D_MODEL), mods, mods)


def _resid_kernel(a_ref, w_ref, x_ref, g_ref, o_ref, *, tm, tn):
    i = pl.program_id(0)
    batch_row = pl.ds(i // (ROWS_B // tm), 1)
    latent = _is_latent_rows(i, tm)
    a = a_ref[...]
    for c in range(tn // MXU_DIM):
        cols = slice(c * MXU_DIM, (c + 1) * MXU_DIM)
        acc = jnp.dot(a, w_ref[:, cols].astype(BF16), preferred_element_type=F32)
        gate = jnp.where(latent, g_ref[batch_row, cols], g_ref[CTX_MOD_ROW:CTX_MOD_ROW + 1, cols])
        o_ref[:, cols] = x_ref[:, cols] + gate * acc


def _resid(a, w_stack, layer, xs, mods, mod_layer, gate_idx, *, name):
    k = w_stack.shape[1]
    if k <= D_MODEL:
        tm, tn = TM_WIDE, 512
    elif k <= 2 * D_MODEL:
        tm, tn = TM_DEEP, 512
    else:
        tm, tn = TM_DEEP, 256
    blocks = _nbytes((tm, k), BF16) + _nbytes((k, tn), F32) + 2 * _nbytes((tm, tn), F32)
    temps = _nbytes((k, tn), BF16) + 2 * _nbytes((tm, tn), F32)
    return pl.pallas_call(
        functools.partial(_resid_kernel, tm=tm, tn=tn),
        grid=(ROWS // tm, D_MODEL // tn),
        in_specs=[pl.BlockSpec((tm, k), lambda i, j: (i, 0)),
                  pl.BlockSpec((None, k, tn), lambda i, j: (layer, 0, j)),
                  pl.BlockSpec((tm, tn), lambda i, j: (i, j)),
                  _mod_spec(mod_layer, gate_idx, tn, lambda i, j: j)],
        out_specs=pl.BlockSpec((tm, tn), lambda i, j: (i, j)),
        out_shape=jax.ShapeDtypeStruct((ROWS, D_MODEL), F32),
        compiler_params=pltpu.CompilerParams(
            dimension_semantics=("arbitrary", "arbitrary"),
            vmem_limit_bytes=_vmem_limit(blocks, temps)),
        name=name,
    )(a, w_stack, xs, mods)


def _attn_scores(q_ref, k_ref, s_ref, m_ref, *, keys, rows_a, tq):
    for m in range(2):
        cols = slice(m * DA_HEAD_DIM, (m + 1) * DA_HEAD_DIM)
        qm = q_ref[:, cols]
        mx = None
        for lo in range(keys[0], keys[1], rows_a):
            rows = slice(lo, lo + rows_a)
            s = lax.dot_general(k_ref[rows, cols], qm, (((1,), (1,)), ((), ())), preferred_element_type=F32)
            s_ref[m, rows, :] = s
            part = jnp.max(s.reshape(rows_a // SUBLANE, SUBLANE, tq), axis=0)
            mx = part if mx is None else jnp.maximum(mx, part)
        m_ref[m] = jnp.broadcast_to(jnp.max(mx, axis=0, keepdims=True), (SUBLANE, tq))


def _attn_finish(lam, vt_ref, g_ref, s_ref, m_ref, o_ref, *, keys, tq, out_scale):
    heads = []
    for m in range(2):
        mrow = m_ref[m][0:1, :]
        acc = jnp.zeros((DA_HEAD_W, tq), F32)
        lsum = jnp.zeros((SUBLANE, tq), F32)
        for lo in range(keys[0], keys[1], MXU_DIM):
            rows = slice(lo, lo + MXU_DIM)
            p = jnp.exp2(s_ref[m, rows, :] - mrow)
            lsum = lsum + jnp.sum(p.reshape(MXU_DIM // SUBLANE, SUBLANE, tq), axis=0)
            acc = acc + jnp.dot(vt_ref[:, rows], p.astype(BF16), preferred_element_type=F32)
        heads.append(acc * (1.0 / jnp.sum(lsum, axis=0, keepdims=True)))
    o = heads[0] - lam * heads[1]
    o = o * lax.rsqrt(jnp.mean(o * o, axis=0, keepdims=True) + EPS) * g_ref[...] * out_scale
    o_ref[...] = o.T.astype(o_ref.dtype)


DA_TQ = CTX_LEN
DA_LATENT_TILES = SEQ // DA_TQ
assert DA_LATENT_TILES % 2 == 0


def _diff_attn_kernel(lam_ref, q_ref, k_ref, v_ref, g_ref, o_ref, vt_ref, sa_ref, sb_ref, ma_ref, mb_ref,
                      *, out_scale):
    t = pl.program_id(2)
    lam = lam_ref[0]
    all_keys, ctx_keys = (0, ROWS_B), (SEQ, ROWS_B)
    scores = functools.partial(_attn_scores, q_ref, k_ref, tq=DA_TQ)
    finish = functools.partial(_attn_finish, lam, vt_ref, g_ref, tq=DA_TQ, out_scale=out_scale)

    @pl.when(t == 0)
    def _():
        for lo in range(0, ROWS_B, MXU_DIM):
            vt_ref[:, lo:lo + MXU_DIM] = v_ref[lo:lo + MXU_DIM, :].astype(F32).T.astype(BF16)
        scores(sa_ref, ma_ref, keys=all_keys, rows_a=TM_DEEP)

    @pl.when(jnp.logical_and(jnp.logical_and(t > 0, t < DA_LATENT_TILES), t % 2 == 0))
    def _():
        scores(sa_ref, ma_ref, keys=all_keys, rows_a=TM_DEEP)
        finish(sb_ref, mb_ref, o_ref, keys=all_keys)

    @pl.when(jnp.logical_and(t < DA_LATENT_TILES, t % 2 == 1))
    def _():
        scores(sb_ref, mb_ref, keys=all_keys, rows_a=TM_DEEP)
        finish(sa_ref, ma_ref, o_ref, keys=all_keys)

    @pl.when(t == DA_LATENT_TILES)
    def _():
        scores(sa_ref, ma_ref, keys=ctx_keys, rows_a=CTX_LEN)
        finish(sb_ref, mb_ref, o_ref, keys=all_keys)

    @pl.when(t == DA_LATENT_TILES + 1)
    def _():
        finish(sa_ref, ma_ref, o_ref, keys=ctx_keys)


def _diff_attn(qkv, lam, subln_g, out_scale):
    tiles_b = ROWS_B // DA_TQ
    kcol0 = DA_WIDTH // DA_HEAD_W
    vcol0 = 2 * kcol0
    blocks = 2 * _nbytes((DA_TQ, DA_HEAD_W), BF16) + 2 * _nbytes((ROWS_B, DA_HEAD_W), BF16)
    scratch = (_nbytes((DA_HEAD_W, ROWS_B), BF16) + 4 * _nbytes((ROWS_B, DA_TQ), F32)
               + 4 * _nbytes((SUBLANE, DA_TQ), F32))
    return pl.pallas_call(
        functools.partial(_diff_attn_kernel, out_scale=out_scale),
        grid=(BATCH, DA_HEADS, DA_LATENT_TILES + 2),
        in_specs=[
            pl.BlockSpec(memory_space=pltpu.SMEM),
            pl.BlockSpec((DA_TQ, DA_HEAD_W),
                         lambda b, h, t: (b * tiles_b + jnp.minimum(t, DA_LATENT_TILES), h)),
            pl.BlockSpec((ROWS_B, DA_HEAD_W), lambda b, h, t: (b, kcol0 + h)),
            pl.BlockSpec((ROWS_B, DA_HEAD_W), lambda b, h, t: (b, vcol0 + h)),
            pl.BlockSpec((DA_HEAD_W, 1), lambda b, h, t: (0, 0)),
        ],
        out_specs=pl.BlockSpec((DA_TQ, DA_HEAD_W),
                               lambda b, h, t: (b * tiles_b + jnp.maximum(t - 1, 0), h)),
        out_shape=jax.ShapeDtypeStruct((ROWS, DA_WIDTH), BF16),
        scratch_shapes=[pltpu.VMEM((DA_HEAD_W, ROWS_B), BF16),
                        pltpu.VMEM((2, ROWS_B, DA_TQ), F32), pltpu.VMEM((2, ROWS_B, DA_TQ), F32),
                        pltpu.VMEM((2, SUBLANE, DA_TQ), F32), pltpu.VMEM((2, SUBLANE, DA_TQ), F32)],
        compiler_params=pltpu.CompilerParams(
            dimension_semantics=("arbitrary", "arbitrary", "arbitrary"),
            vmem_limit_bytes=_vmem_limit(blocks, scratch + 4 * _nbytes((TM_DEEP, DA_TQ), F32))),
        name="diff_attn",
    )(lam, qkv, qkv, qkv, subln_g.reshape(DA_HEAD_W, 1))


def _sg_gate_kernel(z_ref, vg_ref, ws_ref, bs_ref, o_ref, *, chunks):
    v = z_ref[:, SG_WIDTH:]
    v = v * lax.rsqrt(jnp.mean(v * v, axis=-1, keepdims=True) + EPS) * vg_ref[...]
    for c in range(chunks):
        r0 = c * SG_CHUNK
        for g in range(SG_GROUPS):
            c0 = g * SG_GROUP_DIM
            vg = v[r0:r0 + SG_CHUNK, c0:c0 + SG_GROUP_DIM].astype(BF16)
            mixed = jnp.dot(ws_ref[g].astype(BF16), vg, preferred_element_type=F32) + bs_ref[g]
            u = z_ref[r0:r0 + SG_CHUNK, c0:c0 + SG_GROUP_DIM]
            o_ref[r0:r0 + SG_CHUNK, c0:c0 + SG_GROUP_DIM] = (u * mixed).astype(o_ref.dtype)


def _sg_gate(z, v_gain, w_s, b_s):
    chunks = 2
    tm = chunks * SG_CHUNK
    bs_b = jnp.broadcast_to(b_s[:, :, None], (SG_GROUPS, SG_CHUNK, SG_GROUP_DIM))
    blocks = (_nbytes((tm, 2 * SG_WIDTH), F32) + _nbytes((tm, SG_WIDTH), BF16)
              + 2 * _nbytes((SG_GROUPS, SG_CHUNK, SG_CHUNK), F32))
    return pl.pallas_call(
        functools.partial(_sg_gate_kernel, chunks=chunks),
        grid=(ROWS // tm,),
        in_specs=[pl.BlockSpec((tm, 2 * SG_WIDTH), lambda i: (i, 0)),
                  pl.BlockSpec((1, SG_WIDTH), lambda i: (0, 0)),
                  pl.BlockSpec((SG_GROUPS, SG_CHUNK, SG_CHUNK), lambda i: (0, 0, 0)),
                  pl.BlockSpec((SG_GROUPS, SG_CHUNK, SG_GROUP_DIM), lambda i: (0, 0, 0))],
        out_specs=pl.BlockSpec((tm, SG_WIDTH), lambda i: (i, 0)),
        out_shape=jax.ShapeDtypeStruct((ROWS, SG_WIDTH), BF16),
        compiler_params=pltpu.CompilerParams(
            dimension_semantics=("arbitrary",),
            vmem_limit_bytes=_vmem_limit(blocks, 4 * _nbytes((tm, SG_WIDTH), F32))),
        name="sg_gate",
    )(z, v_gain.reshape(1, SG_WIDTH), w_s, bs_b)


def _ret_kernel(*refs, backward):
    if backward:
        g_ref, q_ref, k_ref, v_ref, dec_ref, xi_ref, zeta_ref, of_ref, gp_ref, o_ref, s_ref = refs
    else:
        g_ref, q_ref, k_ref, v_ref, dec_ref, xi_ref, zeta_ref, o_ref, s_ref = refs

    @pl.when(pl.program_id(1) == 0)
    def _():
        s_ref[...] = jnp.zeros_like(s_ref)

    for h in range(RET_HEADS):
        kc = slice(h * RET_KEY_DIM, (h + 1) * RET_KEY_DIM)
        vc = slice(h * RET_VAL_DIM, (h + 1) * RET_VAL_DIM)
        qh = q_ref[:, kc]
        kh = k_ref[:, kc]
        vh = v_ref[:, vc]
        scores = lax.dot_general(qh, kh, (((1,), (1,)), ((), ())), preferred_element_type=F32) * dec_ref[h]
        state = s_ref[h]
        o = jnp.dot(scores.astype(BF16), vh, preferred_element_type=F32)
        cross = jnp.dot(qh, state.astype(BF16), preferred_element_type=F32)
        xi = xi_ref[h]
        o = o + jnp.concatenate(
            [cross[:, e * LANE:(e + 1) * LANE] * xi for e in range(RET_VAL_DIM // LANE)], axis=1)
        zeta = zeta_ref[h]
        kz = jnp.concatenate(
            [kh[:, e * LANE:(e + 1) * LANE].astype(F32) * zeta for e in range(RET_KEY_DIM // LANE)], axis=1)
        kz_t = kz.T.astype(BF16)
        s_ref[h] = g_ref[h] * state + jnp.dot(kz_t, vh, preferred_element_type=F32)
        if backward:
            o = o + of_ref[:, vc]
            o = o * lax.rsqrt(jnp.mean(o * o, axis=-1, keepdims=True) + EPS)
            o_ref[:, vc] = (gp_ref[:, vc] * o).astype(o_ref.dtype)
        else:
            o_ref[:, vc] = o


def _ret_consts(log_gamma, backward):
    pos = jnp.arange(RET_CHUNK, dtype=F32)
    dist = pos[:, None] - pos[None, :]
    if backward:
        dist = -dist
    lg = log_gamma[:, None, None]
    decay = jnp.where(dist >= 0, jnp.exp(lg * jnp.maximum(dist, 0.0)), 0.0)
    pos_in_scan = (RET_CHUNK - 1.0 - pos) if backward else pos
    xi = jnp.exp(log_gamma[:, None] * (pos_in_scan + 1.0))
    zeta = jnp.exp(log_gamma[:, None] * (RET_CHUNK - 1.0 - pos_in_scan))
    g_chunk = jnp.exp(log_gamma * RET_CHUNK)
    bcast = lambda t: jnp.broadcast_to(t[:, :, None], (RET_HEADS, RET_CHUNK, LANE))
    return g_chunk, decay, bcast(xi), bcast(zeta)


def _ret_scan(q, k, v, log_gamma, *, backward, o_fwd=None, gproj=None):
    n_chunks = ROWS_B // RET_CHUNK
    x_chunks = SEQ // RET_CHUNK
    g_chunk, decay, xi, zeta = _ret_consts(log_gamma, backward)
    if backward:
        chunk_of = lambda b, t: (b * n_chunks + (n_chunks - 1 - t), 0)
    else:
        chunk_of = lambda b, t: (b * n_chunks + (t + x_chunks) % n_chunks, 0)
    qk_w = RET_HEADS * RET_KEY_DIM
    v_w = RET_HEADS * RET_VAL_DIM
    const_spec = lambda w: pl.BlockSpec((RET_HEADS, RET_CHUNK, w), lambda b, t: (0, 0, 0))
    in_specs = [pl.BlockSpec(memory_space=pltpu.SMEM),
                pl.BlockSpec((RET_CHUNK, qk_w), chunk_of),
                pl.BlockSpec((RET_CHUNK, qk_w), chunk_of),
                pl.BlockSpec((RET_CHUNK, v_w), chunk_of),
                const_spec(RET_CHUNK), const_spec(LANE), const_spec(LANE)]
    args = [g_chunk, q, k, v, decay, xi, zeta]
    blocks = (2 * _nbytes((RET_CHUNK, qk_w), BF16) + _nbytes((RET_CHUNK, v_w), BF16)
              + _nbytes((RET_HEADS, RET_CHUNK, RET_CHUNK + 2 * LANE), F32)
              + _nbytes((RET_CHUNK, v_w), F32))
    if backward:
        in_specs += [pl.BlockSpec((RET_CHUNK, v_w), chunk_of), pl.BlockSpec((RET_CHUNK, v_w), chunk_of)]
        args += [o_fwd, gproj]
        blocks += 2 * _nbytes((RET_CHUNK, v_w), F32)
        out_dtype = BF16
    else:
        out_dtype = F32
    state_bytes = _nbytes((RET_HEADS, RET_KEY_DIM, RET_VAL_DIM), F32)
    return pl.pallas_call(
        functools.partial(_ret_kernel, backward=backward),
        grid=(BATCH, n_chunks),
        in_specs=in_specs,
        out_specs=pl.BlockSpec((RET_CHUNK, v_w), chunk_of),
        out_shape=jax.ShapeDtypeStruct((ROWS, v_w), out_dtype),
        scratch_shapes=[pltpu.VMEM((RET_HEADS, RET_KEY_DIM, RET_VAL_DIM), F32)],
        compiler_params=pltpu.CompilerParams(
            dimension_semantics=("arbitrary", "arbitrary"),
            vmem_limit_bytes=_vmem_limit(blocks, state_bytes + (8 << 20))),
        name="ret_bwd" if backward else "ret_fwd",
    )(*args)


def _final_norm_kernel(x_ref, g_ref, o_ref):
    x = x_ref[...]
    o_ref[...] = x * lax.rsqrt(jnp.mean(x * x, axis=-1, keepdims=True) + EPS) * g_ref[...]


def _final_norm(xs, gain):
    blocks = 2 * _nbytes((NORM_TM, D_MODEL), F32)
    return pl.pallas_call(
        _final_norm_kernel,
        grid=(BATCH, SEQ // NORM_TM),
        in_specs=[pl.BlockSpec((NORM_TM, D_MODEL), lambda b, t: (b * (ROWS_B // NORM_TM) + t, 0)),
                  pl.BlockSpec((1, D_MODEL), lambda b, t: (0, 0))],
        out_specs=pl.BlockSpec((None, NORM_TM, D_MODEL), lambda b, t: (b, t, 0)),
        out_shape=jax.ShapeDtypeStruct((BATCH, SEQ, D_MODEL), F32),
        compiler_params=pltpu.CompilerParams(
            dimension_semantics=("arbitrary", "arbitrary"),
            vmem_limit_bytes=_vmem_limit(blocks, 2 * _nbytes((NORM_TM, D_MODEL), F32))),
        name="final_norm",
    )(xs, gain.reshape(1, D_MODEL))


def _rope_tables(head_dim):
    rows = SEQ // GRID_W
    row = jnp.broadcast_to(jnp.arange(rows)[:, None], (rows, GRID_W)).reshape(-1).astype(F32)
    col = jnp.broadcast_to(jnp.arange(GRID_W)[None, :], (rows, GRID_W)).reshape(-1).astype(F32)
    n_freq = head_dim // 4
    inv_freq = ROPE_BASE ** (-jnp.arange(n_freq, dtype=F32) / n_freq)
    ang = jnp.concatenate([row[:, None] * inv_freq, col[:, None] * inv_freq], axis=-1)
    cos = jnp.concatenate([jnp.cos(ang), jnp.ones((CTX_LEN, head_dim // 2), F32)], axis=0)
    sin = jnp.concatenate([jnp.sin(ang), jnp.zeros((CTX_LEN, head_dim // 2), F32)], axis=0)
    return cos, sin


def kernel(x, c, ctx, c_ctx, ada_w, ada_b, norm_mix_g, norm_ffn_g, ffn_w_gate_up, ffn_w_down,
           da_w_qkv, da_w_o, da_lambda, da_subln_g,
           sg_w_in, sg_v_g, sg_w_s, sg_b_s, sg_w_out,
           ret_w_q, ret_w_k, ret_w_v, ret_w_g, ret_w_o, ret_decay, final_norm_g):
    cvec = jnp.concatenate([c, c_ctx[None, :], jnp.zeros((MOD_ROWS - BATCH - 1, D_MODEL), F32)], axis=0)
    mods = _ada_tables(cvec, ada_w, ada_b)

    da_cos, da_sin = _rope_tables(DA_HEAD_DIM)
    da_tables = (jnp.concatenate([da_cos, da_cos], axis=1), jnp.concatenate([-da_sin, da_sin], axis=1))
    ret_tables = _rope_tables(RET_KEY_DIM)
    da_q_scale = DA_HEAD_DIM ** -0.5 * math.log2(math.e)

    for i in range(DEPTH):
        kind = i % N_MIXERS
        j = i // N_MIXERS
        if i == 0:
            xs, h = _assemble_norm(x, ctx, norm_mix_g[i], mods, i, 0, 1)
        else:
            h = _normmod(xs, norm_mix_g[i], mods, i, 0, 1)
        ffn_norm = (norm_ffn_g[i], 3, 4)
        if kind == 0:
            lambda_init = 0.8 - 0.6 * math.exp(-0.3 * i)
            lv = da_lambda[j].astype(F32)
            lam = (jnp.exp(jnp.sum(lv[0] * lv[1])) - jnp.exp(jnp.sum(lv[2] * lv[3])) + lambda_init).reshape(1)
            qkv = _proj(h, da_w_qkv, j, epi="rope128", out_dtype=BF16, tables=da_tables,
                        col_scale=da_q_scale, name="da_qkv")
            y = _diff_attn(qkv, lam, da_subln_g[j], 1.0 - lambda_init)
            xs, h = _resid_norm(y, da_w_o, j, xs, mods, i, 2, *ffn_norm, name="da_out")
        elif kind == 1:
            z = _proj(h, sg_w_in, j, epi="gelu", out_dtype=F32, name="sg_in")
            y = _sg_gate(z, sg_v_g[j], sg_w_s[j], sg_b_s[j])
            xs, h = _resid_norm(y, sg_w_out, j, xs, mods, i, 2, *ffn_norm, name="sg_out")
        else:
            log_gamma = -jnp.exp(ret_decay[j].astype(F32))
            q = _proj(h, ret_w_q, j, epi="rope256", out_dtype=BF16, tables=ret_tables, name="ret_q")
            k = _proj(h, ret_w_k, j, epi="rope256", out_dtype=BF16, tables=ret_tables,
                      col_scale=RET_KEY_DIM ** -0.5, name="ret_k")
            v = _proj(h, ret_w_v, j, epi="plain", out_dtype=BF16, name="ret_v")
            gp = _proj(h, ret_w_g, j, epi="silu", out_dtype=F32, name="ret_g")
            o_f = _ret_scan(q, k, v, log_gamma[0], backward=False)
            y = _ret_scan(q, k, v, log_gamma[1], backward=True, o_fwd=o_f, gproj=gp)
            xs = _resid(y, ret_w_o, j, xs, mods, i, 2, name="ret_out")
            h = _normmod(xs, norm_ffn_g[i], mods, i, 3, 4)
        act = _glu(h, ffn_w_gate_up, i)
        xs = _resid(act, ffn_w_down, i, xs, mods, i, 5, name="ffn_down")
    return _final_norm(xs, final_norm_g)
```

```python
import functools
import math

import jax
import jax.numpy as jnp
from jax import lax
from jax.experimental import pallas as pl
from jax.experimental.pallas import tpu as pltpu

D_MODEL = 2048
BATCH = 2
SEQ = 4096
DEPTH = 4
GRID_W = 64
CTX_LEN = 256
N_MIXERS = 3
EPS = 1e-6
ROPE_BASE = 10000.0
N_MOD = 6

DA_HEAD_DIM = 128
DA_HEADS = D_MODEL // (2 * DA_HEAD_DIM)
DA_WIDTH = 2 * DA_HEADS * DA_HEAD_DIM
DA_HEAD_W = 2 * DA_HEAD_DIM

SG_CHUNK = 128
SG_GROUP_DIM = 128
SG_WIDTH = D_MODEL
SG_GROUPS = SG_WIDTH // SG_GROUP_DIM

RET_HEADS = D_MODEL // 256
RET_KEY_DIM = D_MODEL // RET_HEADS
RET_VAL_DIM = 2 * D_MODEL // RET_HEADS
RET_CHUNK = 256

FFN_HIDDEN = -((-8 * D_MODEL) // (3 * 256)) * 256

ROWS_B = SEQ + CTX_LEN
ROWS = BATCH * ROWS_B
TM_WIDE = ROWS_B // 2
TM_DEEP = ROWS_B // 4
NORM_TM = CTX_LEN
NORM_STRIP = 16
MOD_ROWS = 8
CTX_MOD_ROW = BATCH

LANE = 128
SUBLANE = 8
MXU_DIM = 256
VMEM_LIMIT_CAP = 60 * 1024 * 1024
VMEM_LIMIT_FLOOR = 40 * 1024 * 1024

BF16 = jnp.bfloat16
F32 = jnp.float32


def _vmem_limit(block_bytes, temp_bytes):
    return int(max(VMEM_LIMIT_FLOOR, min(VMEM_LIMIT_CAP, 2 * block_bytes + temp_bytes + (4 << 20))))


def _nbytes(shape, dtype):
    return math.prod(shape) * jnp.dtype(dtype).itemsize


def _is_latent_rows(tile_idx, tm):
    row = lax.broadcasted_iota(jnp.int32, (tm, 1), 0) + (tile_idx % (ROWS_B // tm)) * tm
    return row < SEQ


def _silu(x):
    return x * (1.0 / (1.0 + jnp.exp(-x)))


def _ada_kernel(c_ref, w_ref, b_ref, o_ref):
    s = _silu(c_ref[...]).astype(BF16)
    acc = jnp.dot(s, w_ref[...].astype(BF16), preferred_element_type=F32)
    o_ref[...] = acc + b_ref[...]


def _ada_tables(cvec, ada_w, ada_b):
    tn = 1024
    n = N_MOD * D_MODEL
    blocks = _nbytes((D_MODEL, tn), F32) + _nbytes((MOD_ROWS, tn), F32) * 2
    return pl.pallas_call(
        _ada_kernel,
        grid=(DEPTH, n // tn),
        in_specs=[
            pl.BlockSpec((MOD_ROWS, D_MODEL), lambda l, j: (0, 0)),
            pl.BlockSpec((None, D_MODEL, tn), lambda l, j: (l, 0, j)),
            pl.BlockSpec((None, 1, tn), lambda l, j: (l, 0, j)),
        ],
        out_specs=pl.BlockSpec((None, MOD_ROWS, tn), lambda l, j: (l, 0, j)),
        out_shape=jax.ShapeDtypeStruct((DEPTH, MOD_ROWS, n), F32),
        compiler_params=pltpu.CompilerParams(
            dimension_semantics=("arbitrary", "arbitrary"),
            vmem_limit_bytes=_vmem_limit(blocks, _nbytes((D_MODEL, tn), BF16))),
        name="ada_tables",
    )(cvec, ada_w, ada_b.reshape(DEPTH, 1, n))


def _mod_spec(layer, which, width, col_of=lambda *idx: 0):
    blocks_per_vec = D_MODEL // width
    return pl.BlockSpec((None, MOD_ROWS, width), lambda *idx: (layer, 0, which * blocks_per_vec + col_of(*idx)))


def _normmod_kernel(x_ref, g_ref, shift_ref, scale_ref, o_ref):
    i = pl.program_id(0)
    tiles_b = ROWS_B // NORM_TM
    row = pl.ds(jnp.where(i % tiles_b == tiles_b - 1, CTX_MOD_ROW, i // tiles_b), 1)
    mul = g_ref[...] * (1.0 + scale_ref[row, :])
    add = shift_ref[row, :]
    for r in range(NORM_TM // NORM_STRIP):
        rows = slice(r * NORM_STRIP, (r + 1) * NORM_STRIP)
        x = x_ref[rows, :]
        inv = lax.rsqrt(jnp.mean(x * x, axis=-1, keepdims=True) + EPS)
        o_ref[rows, :] = (x * inv * mul + add).astype(o_ref.dtype)


def _normmod(xs, gain, mods, layer, shift_idx, scale_idx):
    blocks = _nbytes((NORM_TM, D_MODEL), F32) + _nbytes((NORM_TM, D_MODEL), BF16)
    return pl.pallas_call(
        _normmod_kernel,
        grid=(ROWS // NORM_TM,),
        in_specs=[pl.BlockSpec((NORM_TM, D_MODEL), lambda i: (i, 0)),
                  pl.BlockSpec((1, D_MODEL), lambda i: (0, 0)),
                  _mod_spec(layer, shift_idx, D_MODEL), _mod_spec(layer, scale_idx, D_MODEL)],
        out_specs=pl.BlockSpec((NORM_TM, D_MODEL), lambda i: (i, 0)),
        out_shape=jax.ShapeDtypeStruct((ROWS, D_MODEL), BF16),
        compiler_params=pltpu.CompilerParams(
            dimension_semantics=("arbitrary",),
            vmem_limit_bytes=_vmem_limit(blocks, 0)),
        name="normmod",
    )(xs, gain.reshape(1, D_MODEL), mods, mods)


def _assemble_norm_kernel(x_ref, ctx_ref, g_ref, shift_ref, scale_ref, xs_ref, h_ref):
    b, t = pl.program_id(0), pl.program_id(1)
    is_ctx = t == SEQ // NORM_TM
    row = pl.ds(jnp.where(is_ctx, CTX_MOD_ROW, b), 1)
    mul = g_ref[...] * (1.0 + scale_ref[row, :])
    add = shift_ref[row, :]

    def emit(src_ref):
        for r in range(NORM_TM // NORM_STRIP):
            rows = slice(r * NORM_STRIP, (r + 1) * NORM_STRIP)
            x = src_ref[rows, :]
            xs_ref[rows, :] = x
            inv = lax.rsqrt(jnp.mean(x * x, axis=-1, keepdims=True) + EPS)
            h_ref[rows, :] = (x * inv * mul + add).astype(h_ref.dtype)

    @pl.when(jnp.logical_not(is_ctx))
    def _():
        emit(x_ref)

    @pl.when(is_ctx)
    def _():
        emit(ctx_ref)


def _assemble_norm(x, ctx, gain, mods, layer, shift_idx, scale_idx):
    tiles_b = ROWS_B // NORM_TM
    x_tiles = SEQ // NORM_TM
    out_spec = lambda: pl.BlockSpec((NORM_TM, D_MODEL), lambda b, t: (b * tiles_b + t, 0))
    blocks = 3 * _nbytes((NORM_TM, D_MODEL), F32) + _nbytes((NORM_TM, D_MODEL), BF16)
    return pl.pallas_call(
        _assemble_norm_kernel,
        grid=(BATCH, tiles_b),
        in_specs=[pl.BlockSpec((None, NORM_TM, D_MODEL), lambda b, t: (b, jnp.minimum(t, x_tiles - 1), 0)),
                  pl.BlockSpec((None, CTX_LEN, D_MODEL), lambda b, t: (b, 0, 0)),
                  pl.BlockSpec((1, D_MODEL), lambda b, t: (0, 0)),
                  _mod_spec(layer, shift_idx, D_MODEL), _mod_spec(layer, scale_idx, D_MODEL)],
        out_specs=[out_spec(), out_spec()],
        out_shape=[jax.ShapeDtypeStruct((ROWS, D_MODEL), F32), jax.ShapeDtypeStruct((ROWS, D_MODEL), BF16)],
        compiler_params=pltpu.CompilerParams(
            dimension_semantics=("arbitrary", "arbitrary"),
            vmem_limit_bytes=_vmem_limit(blocks, 0)),
        name="assemble_norm",
    )(x, ctx, gain.reshape(1, D_MODEL), mods, mods)


def _rope_half_roll(acc, cos2, sin2):
    return acc * cos2 + pltpu.roll(acc, DA_HEAD_DIM // 2, 1) * sin2


def _proj_chunks(a, w_refs, table_refs, o_ref, *, epi, tn, col_scale):
    if epi == "rope256":
        cos, sin = table_refs[0][...] * col_scale, table_refs[1][...] * col_scale
    elif epi == "rope128":
        j = pl.program_id(1)
        tiles_per_part = DA_WIDTH // tn
        rotated = j < 2 * tiles_per_part
        scale = jnp.where(j < tiles_per_part, col_scale, 1.0)
        cos = jnp.where(rotated, table_refs[0][...] * scale, 1.0)
        sin = jnp.where(rotated, table_refs[1][...] * scale, 0.0)
    for c in range(tn // MXU_DIM):
        cols = slice(c * MXU_DIM, (c + 1) * MXU_DIM)
        acc = jnp.dot(a, w_refs[0][:, cols].astype(BF16), preferred_element_type=F32)
        if epi == "plain":
            out = acc
        elif epi == "silu":
            out = _silu(acc)
        elif epi == "gelu":
            out = 0.5 * acc * (1.0 + lax.erf(acc * (2.0 ** -0.5)))
        elif epi == "glu":
            out = _silu(acc) * jnp.dot(a, w_refs[1][:, cols].astype(BF16), preferred_element_type=F32)
        elif epi == "rope256":
            x1, x2 = acc[:, :LANE], acc[:, LANE:]
            out = jnp.concatenate([x1 * cos - x2 * sin, x1 * sin + x2 * cos], axis=1)
        elif epi == "rope128":
            out = jnp.concatenate(
                [_rope_half_roll(acc[:, g * LANE:(g + 1) * LANE], cos, sin) for g in range(MXU_DIM // LANE)],
                axis=1)
        else:
            raise ValueError(epi)
        o_ref[:, cols] = out.astype(o_ref.dtype)


def _proj_kernel(*refs, epi, tn, col_scale):
    a_ref, w_ref, *table_refs, o_ref = refs
    _proj_chunks(a_ref[...], (w_ref,), table_refs, o_ref, epi=epi, tn=tn, col_scale=col_scale)


def _proj(a, w_stack, layer, *, epi, out_dtype, tables=None, col_scale=1.0, name):
    _, k, n = w_stack.shape
    tm = TM_WIDE
    tn = 1024 if jnp.dtype(out_dtype).itemsize == 2 else 512
    in_specs = [pl.BlockSpec((tm, k), lambda i, j: (i, 0)),
                pl.BlockSpec((None, k, tn), lambda i, j: (layer, 0, j))]
    args = [a, w_stack]
    if tables is not None:
        in_specs += [pl.BlockSpec((tm, LANE), lambda i, j: (i % (ROWS_B // tm), 0))] * 2
        args += list(tables)
    blocks = (_nbytes((tm, k), BF16) + _nbytes((k, tn), F32) + _nbytes((tm, tn), out_dtype)
              + 2 * _nbytes((tm, LANE), F32))
    temps = _nbytes((k, tn), BF16) + 2 * _nbytes((tm, tn), F32)
    return pl.pallas_call(
        functools.partial(_proj_kernel, epi=epi, tn=tn, col_scale=col_scale),
        grid=(ROWS // tm, n // tn),
        in_specs=in_specs,
        out_specs=pl.BlockSpec((tm, tn), lambda i, j: (i, j)),
        out_shape=jax.ShapeDtypeStruct((ROWS, n), out_dtype),
        compiler_params=pltpu.CompilerParams(
            dimension_semantics=("arbitrary", "arbitrary"),
            vmem_limit_bytes=_vmem_limit(blocks, temps)),
        name=name,
    )(*args)


def _glu_kernel(a_ref, wg_ref, wu_ref, o_ref, *, tn):
    _proj_chunks(a_ref[...], (wg_ref, wu_ref), (), o_ref, epi="glu", tn=tn, col_scale=1.0)


def _glu(a, w_stack, layer):
    k = a.shape[1]
    tm, tn = TM_WIDE, 512
    nt = FFN_HIDDEN // tn
    blocks = _nbytes((tm, k), BF16) + 2 * _nbytes((k, tn), F32) + _nbytes((tm, tn), BF16)
    temps = 2 * _nbytes((k, tn), BF16) + 3 * _nbytes((tm, tn), F32)
    return pl.pallas_call(
        functools.partial(_glu_kernel, tn=tn),
        grid=(ROWS // tm, nt),
        in_specs=[pl.BlockSpec((tm, k), lambda i, j: (i, 0)),
                  pl.BlockSpec((None, k, tn), lambda i, j: (layer, 0, j)),
                  pl.BlockSpec((None, k, tn), lambda i, j: (layer, 0, j + nt))],
        out_specs=pl.BlockSpec((tm, tn), lambda i, j: (i, j)),
        out_shape=jax.ShapeDtypeStruct((ROWS, FFN_HIDDEN), BF16),
        compiler_params=pltpu.CompilerParams(
            dimension_semantics=("arbitrary", "arbitrary"),
            vmem_limit_bytes=_vmem_limit(blocks, temps)),
        name="ffn_glu",
    )(a, w_stack, w_stack)


RN_TM = ROWS_B // 8


def _resid_norm_kernel(a_ref, w_ref, x_ref, gate_ref, gain_ref, shift_ref, scale_ref, xo_ref, h_ref):
    i = pl.program_id(0)
    tiles_b = ROWS_B // RN_TM
    batch_row = pl.ds(i // tiles_b, 1)
    ctx_row = slice(CTX_MOD_ROW, CTX_MOD_ROW + 1)

    def body(pick):
        a = a_ref[...]
        ssq = jnp.zeros((RN_TM, 1), F32)
        for c in range(D_MODEL // MXU_DIM):
            cols = slice(c * MXU_DIM, (c + 1) * MXU_DIM)
            acc = jnp.dot(a, w_ref[:, cols].astype(BF16), preferred_element_type=F32)
            xn = x_ref[:, cols] + pick(gate_ref, cols) * acc
            xo_ref[:, cols] = xn
            ssq = ssq + jnp.sum(xn * xn, axis=-1, keepdims=True)
        inv = lax.rsqrt(ssq * (1.0 / D_MODEL) + EPS)
        for c in range(D_MODEL // MXU_DIM):
            cols = slice(c * MXU_DIM, (c + 1) * MXU_DIM)
            mul = gain_ref[:, cols] * (1.0 + pick(scale_ref, cols))
            h_ref[:, cols] = (xo_ref[:, cols] * inv * mul + pick(shift_ref, cols)).astype(h_ref.dtype)

    assert (tiles_b - 1) * RN_TM <= SEQ
    mixed = i % tiles_b == tiles_b - 1

    @pl.when(mixed)
    def _():
        latent = _is_latent_rows(i, RN_TM)
        body(lambda ref, cols: jnp.where(latent, ref[batch_row, cols], ref[ctx_row, cols]))

    @pl.when(jnp.logical_not(mixed))
    def _():
        body(lambda ref, cols: ref[batch_row, cols])


def _resid_norm(a, w_stack, layer, xs, mods, mod_layer, gate_idx, gain, shift_idx, scale_idx, *, name):
    k = w_stack.shape[1]
    assert k == D_MODEL
    row_spec = lambda: pl.BlockSpec((RN_TM, D_MODEL), lambda i: (i, 0))
    blocks = 2 * _nbytes((RN_TM, D_MODEL), BF16) + 2 * _nbytes((RN_TM, D_MODEL), F32)
    resident = _nbytes((k, D_MODEL), F32)
    temps = 2 * _nbytes((k, MXU_DIM), BF16) + 4 * _nbytes((RN_TM, MXU_DIM), F32)
    return pl.pallas_call(
        _resid_norm_kernel,
        grid=(ROWS // RN_TM,),
        in_specs=[row_spec(),
                  pl.BlockSpec((None, k, D_MODEL), lambda i: (layer, 0, 0), pipeline_mode=pl.Buffered(1)),
                  row_spec(),
                  _mod_spec(mod_layer, gate_idx, D_MODEL),
                  pl.BlockSpec((1, D_MODEL), lambda i: (0, 0)),
                  _mod_spec(mod_layer, shift_idx, D_MODEL), _mod_spec(mod_layer, scale_idx, D_MODEL)],
        out_specs=[row_spec(), row_spec()],
        out_shape=[jax.ShapeDtypeStruct((ROWS, D_MODEL), F32), jax.ShapeDtypeStruct((ROWS, D_MODEL), BF16)],
        compiler_params=pltpu.CompilerParams(
            dimension_semantics=("arbitrary",),
            vmem_limit_bytes=_vmem_limit(blocks, resident + temps)),
        name=name,
    )(a, w_stack, xs, mods, gain.reshape(1, D_MODEL), mods, mods)


def _resid_kernel(a_ref, w_ref, x_ref, g_ref, o_ref, *, tm, tn):
    i = pl.program_id(0)
    batch_row = pl.ds(i // (ROWS_B // tm), 1)
    latent = _is_latent_rows(i, tm)
    a = a_ref[...]
    for c in range(tn // MXU_DIM):
        cols = slice(c * MXU_DIM, (c + 1) * MXU_DIM)
        acc = jnp.dot(a, w_ref[:, cols].astype(BF16), preferred_element_type=F32)
        gate = jnp.where(latent, g_ref[batch_row, cols], g_ref[CTX_MOD_ROW:CTX_MOD_ROW + 1, cols])
        o_ref[:, cols] = x_ref[:, cols] + gate * acc


def _resid(a, w_stack, layer, xs, mods, mod_layer, gate_idx, *, name):
    k = w_stack.shape[1]
    if k <= D_MODEL:
        tm, tn = TM_WIDE, 512
    elif k <= 2 * D_MODEL:
        tm, tn = TM_DEEP, 512
    else:
        tm, tn = TM_DEEP, 256
    blocks = _nbytes((tm, k), BF16) + _nbytes((k, tn), F32) + 2 * _nbytes((tm, tn), F32)
    temps = _nbytes((k, tn), BF16) + 2 * _nbytes((tm, tn), F32)
    return pl.pallas_call(
        functools.partial(_resid_kernel, tm=tm, tn=tn),
        grid=(ROWS // tm, D_MODEL // tn),
        in_specs=[pl.BlockSpec((tm, k), lambda i, j: (i, 0)),
                  pl.BlockSpec((None, k, tn), lambda i, j: (layer, 0, j)),
                  pl.BlockSpec((tm, tn), lambda i, j: (i, j)),
                  _mod_spec(mod_layer, gate_idx, tn, lambda i, j: j)],
        out_specs=pl.BlockSpec((tm, tn), lambda i, j: (i, j)),
        out_shape=jax.ShapeDtypeStruct((ROWS, D_MODEL), F32),
        compiler_params=pltpu.CompilerParams(
            dimension_semantics=("arbitrary", "arbitrary"),
            vmem_limit_bytes=_vmem_limit(blocks, temps)),
        name=name,
    )(a, w_stack, xs, mods)


def _attn_scores(q_ref, k_ref, s_ref, m_ref, *, keys, rows_a, tq):
    for m in range(2):
        cols = slice(m * DA_HEAD_DIM, (m + 1) * DA_HEAD_DIM)
        qm = q_ref[:, cols]
        mx = None
        for lo in range(keys[0], keys[1], rows_a):
            rows = slice(lo, lo + rows_a)
            s = lax.dot_general(k_ref[rows, cols], qm, (((1,), (1,)), ((), ())), preferred_element_type=F32)
            s_ref[m, rows, :] = s
            part = jnp.max(s.reshape(rows_a // SUBLANE, SUBLANE, tq), axis=0)
            mx = part if mx is None else jnp.maximum(mx, part)
        m_ref[m] = jnp.broadcast_to(jnp.max(mx, axis=0, keepdims=True), (SUBLANE, tq))


def _attn_finish(lam, vt_ref, g_ref, s_ref, m_ref, o_ref, *, keys, tq, out_scale):
    heads = []
    for m in range(2):
        mrow = m_ref[m][0:1, :]
        acc = jnp.zeros((DA_HEAD_W, tq), F32)
        lsum = jnp.zeros((SUBLANE, tq), F32)
        for lo in range(keys[0], keys[1], MXU_DIM):
            rows = slice(lo, lo + MXU_DIM)
            p = jnp.exp2(s_ref[m, rows, :] - mrow)
            lsum = lsum + jnp.sum(p.reshape(MXU_DIM // SUBLANE, SUBLANE, tq), axis=0)
            acc = acc + jnp.dot(vt_ref[:, rows], p.astype(BF16), preferred_element_type=F32)
        heads.append(acc * (1.0 / jnp.sum(lsum, axis=0, keepdims=True)))
    o = heads[0] - lam * heads[1]
    o = o * lax.rsqrt(jnp.mean(o * o, axis=0, keepdims=True) + EPS) * g_ref[...] * out_scale
    o_ref[...] = o.T.astype(o_ref.dtype)


DA_TQ = CTX_LEN
DA_LATENT_TILES = SEQ // DA_TQ
assert DA_LATENT_TILES % 2 == 0


def _diff_attn_kernel(lam_ref, q_ref, k_ref, v_ref, g_ref, o_ref, vt_ref, sa_ref, sb_ref, ma_ref, mb_ref,
                      *, out_scale):
    t = pl.program_id(2)
    lam = lam_ref[0]
    all_keys, ctx_keys = (0, ROWS_B), (SEQ, ROWS_B)
    scores = functools.partial(_attn_scores, q_ref, k_ref, tq=DA_TQ)
    finish = functools.partial(_attn_finish, lam, vt_ref, g_ref, tq=DA_TQ, out_scale=out_scale)

    @pl.when(t == 0)
    def _():
        for lo in range(0, ROWS_B, MXU_DIM):
            vt_ref[:, lo:lo + MXU_DIM] = v_ref[lo:lo + MXU_DIM, :].astype(F32).T.astype(BF16)
        scores(sa_ref, ma_ref, keys=all_keys, rows_a=TM_DEEP)

    @pl.when(jnp.logical_and(jnp.logical_and(t > 0, t < DA_LATENT_TILES), t % 2 == 0))
    def _():
        scores(sa_ref, ma_ref, keys=all_keys, rows_a=TM_DEEP)
        finish(sb_ref, mb_ref, o_ref, keys=all_keys)

    @pl.when(jnp.logical_and(t < DA_LATENT_TILES, t % 2 == 1))
    def _():
        scores(sb_ref, mb_ref, keys=all_keys, rows_a=TM_DEEP)
        finish(sa_ref, ma_ref, o_ref, keys=all_keys)

    @pl.when(t == DA_LATENT_TILES)
    def _():
        scores(sa_ref, ma_ref, keys=ctx_keys, rows_a=CTX_LEN)
        finish(sb_ref, mb_ref, o_ref, keys=all_keys)

    @pl.when(t == DA_LATENT_TILES + 1)
    def _():
        finish(sa_ref, ma_ref, o_ref, keys=ctx_keys)


def _diff_attn(qkv, lam, subln_g, out_scale):
    tiles_b = ROWS_B // DA_TQ
    kcol0 = DA_WIDTH // DA_HEAD_W
    vcol0 = 2 * kcol0
    blocks = 2 * _nbytes((DA_TQ, DA_HEAD_W), BF16) + 2 * _nbytes((ROWS_B, DA_HEAD_W), BF16)
    scratch = (_nbytes((DA_HEAD_W, ROWS_B), BF16) + 4 * _nbytes((ROWS_B, DA_TQ), F32)
               + 4 * _nbytes((SUBLANE, DA_TQ), F32))
    return pl.pallas_call(
        functools.partial(_diff_attn_kernel, out_scale=out_scale),
        grid=(BATCH, DA_HEADS, DA_LATENT_TILES + 2),
        in_specs=[
            pl.BlockSpec(memory_space=pltpu.SMEM),
            pl.BlockSpec((DA_TQ, DA_HEAD_W),
                         lambda b, h, t: (b * tiles_b + jnp.minimum(t, DA_LATENT_TILES), h)),
            pl.BlockSpec((ROWS_B, DA_HEAD_W), lambda b, h, t: (b, kcol0 + h)),
            pl.BlockSpec((ROWS_B, DA_HEAD_W), lambda b, h, t: (b, vcol0 + h)),
            pl.BlockSpec((DA_HEAD_W, 1), lambda b, h, t: (0, 0)),
        ],
        out_specs=pl.BlockSpec((DA_TQ, DA_HEAD_W),
                               lambda b, h, t: (b * tiles_b + jnp.maximum(t - 1, 0), h)),
        out_shape=jax.ShapeDtypeStruct((ROWS, DA_WIDTH), BF16),
        scratch_shapes=[pltpu.VMEM((DA_HEAD_W, ROWS_B), BF16),
                        pltpu.VMEM((2, ROWS_B, DA_TQ), F32), pltpu.VMEM((2, ROWS_B, DA_TQ), F32),
                        pltpu.VMEM((2, SUBLANE, DA_TQ), F32), pltpu.VMEM((2, SUBLANE, DA_TQ), F32)],
        compiler_params=pltpu.CompilerParams(
            dimension_semantics=("arbitrary", "arbitrary", "arbitrary"),
            vmem_limit_bytes=_vmem_limit(blocks, scratch + 4 * _nbytes((TM_DEEP, DA_TQ), F32))),
        name="diff_attn",
    )(lam, qkv, qkv, qkv, subln_g.reshape(DA_HEAD_W, 1))


def _sg_gate_kernel(z_ref, vg_ref, ws_ref, bs_ref, o_ref, *, chunks):
    v = z_ref[:, SG_WIDTH:]
    v = v * lax.rsqrt(jnp.mean(v * v, axis=-1, keepdims=True) + EPS) * vg_ref[...]
    for c in range(chunks):
        r0 = c * SG_CHUNK
        for g in range(SG_GROUPS):
            c0 = g * SG_GROUP_DIM
            vg = v[r0:r0 + SG_CHUNK, c0:c0 + SG_GROUP_DIM].astype(BF16)
            mixed = jnp.dot(ws_ref[g].astype(BF16), vg, preferred_element_type=F32) + bs_ref[g]
            u = z_ref[r0:r0 + SG_CHUNK, c0:c0 + SG_GROUP_DIM]
            o_ref[r0:r0 + SG_CHUNK, c0:c0 + SG_GROUP_DIM] = (u * mixed).astype(o_ref.dtype)


def _sg_gate(z, v_gain, w_s, b_s):
    chunks = 2
    tm = chunks * SG_CHUNK
    bs_b = jnp.broadcast_to(b_s[:, :, None], (SG_GROUPS, SG_CHUNK, SG_GROUP_DIM))
    blocks = (_nbytes((tm, 2 * SG_WIDTH), F32) + _nbytes((tm, SG_WIDTH), BF16)
              + 2 * _nbytes((SG_GROUPS, SG_CHUNK, SG_CHUNK), F32))
    return pl.pallas_call(
        functools.partial(_sg_gate_kernel, chunks=chunks),
        grid=(ROWS // tm,),
        in_specs=[pl.BlockSpec((tm, 2 * SG_WIDTH), lambda i: (i, 0)),
                  pl.BlockSpec((1, SG_WIDTH), lambda i: (0, 0)),
                  pl.BlockSpec((SG_GROUPS, SG_CHUNK, SG_CHUNK), lambda i: (0, 0, 0)),
                  pl.BlockSpec((SG_GROUPS, SG_CHUNK, SG_GROUP_DIM), lambda i: (0, 0, 0))],
        out_specs=pl.BlockSpec((tm, SG_WIDTH), lambda i: (i, 0)),
        out_shape=jax.ShapeDtypeStruct((ROWS, SG_WIDTH), BF16),
        compiler_params=pltpu.CompilerParams(
            dimension_semantics=("arbitrary",),
            vmem_limit_bytes=_vmem_limit(blocks, 4 * _nbytes((tm, SG_WIDTH), F32))),
        name="sg_gate",
    )(z, v_gain.reshape(1, SG_WIDTH), w_s, bs_b)


def _ret_kernel(*refs, backward):
    if backward:
        g_ref, q_ref, k_ref, v_ref, dec_ref, xi_ref, zeta_ref, of_ref, gp_ref, o_ref, s_ref = refs
    else:
        g_ref, q_ref, k_ref, v_ref, dec_ref, xi_ref, zeta_ref, o_ref, s_ref = refs

    @pl.when(pl.program_id(1) == 0)
    def _():
        s_ref[...] = jnp.zeros_like(s_ref)

    for h in range(RET_HEADS):
        kc = slice(h * RET_KEY_DIM, (h + 1) * RET_KEY_DIM)
        vc = slice(h * RET_VAL_DIM, (h + 1) * RET_VAL_DIM)
        qh = q_ref[:, kc]
        kh = k_ref[:, kc]
        vh = v_ref[:, vc]
        scores = lax.dot_general(qh, kh, (((1,), (1,)), ((), ())), preferred_element_type=F32) * dec_ref[h]
        state = s_ref[h]
        o = jnp.dot(scores.astype(BF16), vh, preferred_element_type=F32)
        cross = jnp.dot(qh, state.astype(BF16), preferred_element_type=F32)
        xi = xi_ref[h]
        o = o + jnp.concatenate(
            [cross[:, e * LANE:(e + 1) * LANE] * xi for e in range(RET_VAL_DIM // LANE)], axis=1)
        zeta = zeta_ref[h]
        kz = jnp.concatenate(
            [kh[:, e * LANE:(e + 1) * LANE].astype(F32) * zeta for e in range(RET_KEY_DIM // LANE)], axis=1)
        kz_t = kz.T.astype(BF16)
        s_ref[h] = g_ref[h] * state + jnp.dot(kz_t, vh, preferred_element_type=F32)
        if backward:
            o = o + of_ref[:, vc]
            o = o * lax.rsqrt(jnp.mean(o * o, axis=-1, keepdims=True) + EPS)
            o_ref[:, vc] = (gp_ref[:, vc] * o).astype(o_ref.dtype)
        else:
            o_ref[:, vc] = o


def _ret_consts(log_gamma, backward):
    pos = jnp.arange(RET_CHUNK, dtype=F32)
    dist = pos[:, None] - pos[None, :]
    if backward:
        dist = -dist
    lg = log_gamma[:, None, None]
    decay = jnp.where(dist >= 0, jnp.exp(lg * jnp.maximum(dist, 0.0)), 0.0)
    pos_in_scan = (RET_CHUNK - 1.0 - pos) if backward else pos
    xi = jnp.exp(log_gamma[:, None] * (pos_in_scan + 1.0))
    zeta = jnp.exp(log_gamma[:, None] * (RET_CHUNK - 1.0 - pos_in_scan))
    g_chunk = jnp.exp(log_gamma * RET_CHUNK)
    bcast = lambda t: jnp.broadcast_to(t[:, :, None], (RET_HEADS, RET_CHUNK, LANE))
    return g_chunk, decay, bcast(xi), bcast(zeta)


def _ret_scan(q, k, v, log_gamma, *, backward, o_fwd=None, gproj=None):
    n_chunks = ROWS_B // RET_CHUNK
    x_chunks = SEQ // RET_CHUNK
    g_chunk, decay, xi, zeta = _ret_consts(log_gamma, backward)
    if backward:
        chunk_of = lambda b, t: (b * n_chunks + (n_chunks - 1 - t), 0)
    else:
        chunk_of = lambda b, t: (b * n_chunks + (t + x_chunks) % n_chunks, 0)
    qk_w = RET_HEADS * RET_KEY_DIM
    v_w = RET_HEADS * RET_VAL_DIM
    const_spec = lambda w: pl.BlockSpec((RET_HEADS, RET_CHUNK, w), lambda b, t: (0, 0, 0))
    in_specs = [pl.BlockSpec(memory_space=pltpu.SMEM),
                pl.BlockSpec((RET_CHUNK, qk_w), chunk_of),
                pl.BlockSpec((RET_CHUNK, qk_w), chunk_of),
                pl.BlockSpec((RET_CHUNK, v_w), chunk_of),
                const_spec(RET_CHUNK), const_spec(LANE), const_spec(LANE)]
    args = [g_chunk, q, k, v, decay, xi, zeta]
    blocks = (2 * _nbytes((RET_CHUNK, qk_w), BF16) + _nbytes((RET_CHUNK, v_w), BF16)
              + _nbytes((RET_HEADS, RET_CHUNK, RET_CHUNK + 2 * LANE), F32)
              + _nbytes((RET_CHUNK, v_w), F32))
    if backward:
        in_specs += [pl.BlockSpec((RET_CHUNK, v_w), chunk_of), pl.BlockSpec((RET_CHUNK, v_w), chunk_of)]
        args += [o_fwd, gproj]
        blocks += 2 * _nbytes((RET_CHUNK, v_w), F32)
        out_dtype = BF16
    else:
        out_dtype = F32
    state_bytes = _nbytes((RET_HEADS, RET_KEY_DIM, RET_VAL_DIM), F32)
    return pl.pallas_call(
        functools.partial(_ret_kernel, backward=backward),
        grid=(BATCH, n_chunks),
        in_specs=in_specs,
        out_specs=pl.BlockSpec((RET_CHUNK, v_w), chunk_of),
        out_shape=jax.ShapeDtypeStruct((ROWS, v_w), out_dtype),
        scratch_shapes=[pltpu.VMEM((RET_HEADS, RET_KEY_DIM, RET_VAL_DIM), F32)],
        compiler_params=pltpu.CompilerParams(
            dimension_semantics=("arbitrary", "arbitrary"),
            vmem_limit_bytes=_vmem_limit(blocks, state_bytes + (8 << 20))),
        name="ret_bwd" if backward else "ret_fwd",
    )(*args)


def _final_norm_kernel(x_ref, g_ref, o_ref):
    x = x_ref[...]
    o_ref[...] = x * lax.rsqrt(jnp.mean(x * x, axis=-1, keepdims=True) + EPS) * g_ref[...]


def _final_norm(xs, gain):
    blocks = 2 * _nbytes((NORM_TM, D_MODEL), F32)
    return pl.pallas_call(
        _final_norm_kernel,
        grid=(BATCH, SEQ // NORM_TM),
        in_specs=[pl.BlockSpec((NORM_TM, D_MODEL), lambda b, t: (b * (ROWS_B // NORM_TM) + t, 0)),
                  pl.BlockSpec((1, D_MODEL), lambda b, t: (0, 0))],
        out_specs=pl.BlockSpec((None, NORM_TM, D_MODEL), lambda b, t: (b, t, 0)),
        out_shape=jax.ShapeDtypeStruct((BATCH, SEQ, D_MODEL), F32),
        compiler_params=pltpu.CompilerParams(
            dimension_semantics=("arbitrary", "arbitrary"),
            vmem_limit_bytes=_vmem_limit(blocks, 2 * _nbytes((NORM_TM, D_MODEL), F32))),
        name="final_norm",
    )(xs, gain.reshape(1, D_MODEL))


def _rope_tables(head_dim):
    rows = SEQ // GRID_W
    row = jnp.broadcast_to(jnp.arange(rows)[:, None], (rows, GRID_W)).reshape(-1).astype(F32)
    col = jnp.broadcast_to(jnp.arange(GRID_W)[None, :], (rows, GRID_W)).reshape(-1).astype(F32)
    n_freq = head_dim // 4
    inv_freq = ROPE_BASE ** (-jnp.arange(n_freq, dtype=F32) / n_freq)
    ang = jnp.concatenate([row[:, None] * inv_freq, col[:, None] * inv_freq], axis=-1)
    cos = jnp.concatenate([jnp.cos(ang), jnp.ones((CTX_LEN, head_dim // 2), F32)], axis=0)
    sin = jnp.concatenate([jnp.sin(ang), jnp.zeros((CTX_LEN, head_dim // 2), F32)], axis=0)
    return cos, sin


def kernel(x, c, ctx, c_ctx, ada_w, ada_b, norm_mix_g, norm_ffn_g, ffn_w_gate_up, ffn_w_down,
           da_w_qkv, da_w_o, da_lambda, da_subln_g,
           sg_w_in, sg_v_g, sg_w_s, sg_b_s, sg_w_out,
           ret_w_q, ret_w_k, ret_w_v, ret_w_g, ret_w_o, ret_decay, final_norm_g):
    cvec = jnp.concatenate([c, c_ctx[None, :], jnp.zeros((MOD_ROWS - BATCH - 1, D_MODEL), F32)], axis=0)
    mods = _ada_tables(cvec, ada_w, ada_b)

    da_cos, da_sin = _rope_tables(DA_HEAD_DIM)
    da_tables = (jnp.concatenate([da_cos, da_cos], axis=1), jnp.concatenate([-da_sin, da_sin], axis=1))
    ret_tables = _rope_tables(RET_KEY_DIM)
    da_q_scale = DA_HEAD_DIM ** -0.5 * math.log2(math.e)

    for i in range(DEPTH):
        kind = i % N_MIXERS
        j = i // N_MIXERS
        if i == 0:
            xs, h = _assemble_norm(x, ctx, norm_mix_g[i], mods, i, 0, 1)
        else:
            h = _normmod(xs, norm_mix_g[i], mods, i, 0, 1)
        ffn_norm = (norm_ffn_g[i], 3, 4)
        if kind == 0:
            lambda_init = 0.8 - 0.6 * math.exp(-0.3 * i)
            lv = da_lambda[j].astype(F32)
            lam = (jnp.exp(jnp.sum(lv[0] * lv[1])) - jnp.exp(jnp.sum(lv[2] * lv[3])) + lambda_init).reshape(1)
            qkv = _proj(h, da_w_qkv, j, epi="rope128", out_dtype=BF16, tables=da_tables,
                        col_scale=da_q_scale, name="da_qkv")
            y = _diff_attn(qkv, lam, da_subln_g[j], 1.0 - lambda_init)
            xs, h = _resid_norm(y, da_w_o, j, xs, mods, i, 2, *ffn_norm, name="da_out")
        elif kind == 1:
            z = _proj(h, sg_w_in, j, epi="gelu", out_dtype=F32, name="sg_in")
            y = _sg_gate(z, sg_v_g[j], sg_w_s[j], sg_b_s[j])
            xs, h = _resid_norm(y, sg_w_out, j, xs, mods, i, 2, *ffn_norm, name="sg_out")
        else:
            log_gamma = -jnp.exp(ret_decay[j].astype(F32))
            q = _proj(h, ret_w_q, j, epi="rope256", out_dtype=BF16, tables=ret_tables, name="ret_q")
            k = _proj(h, ret_w_k, j, epi="rope256", out_dtype=BF16, tables=ret_tables,
                      col_scale=RET_KEY_DIM ** -0.5, name="ret_k")
            v = _proj(h, ret_w_v, j, epi="plain", out_dtype=BF16, name="ret_v")
            gp = _proj(h, ret_w_g, j, epi="silu", out_dtype=F32, name="ret_g")
            o_f = _ret_scan(q, k, v, log_gamma[0], backward=False)
            y = _ret_scan(q, k, v, log_gamma[1], backward=True, o_fwd=o_f, gproj=gp)
            xs = _resid(y, ret_w_o, j, xs, mods, i, 2, name="ret_out")
            h = _normmod(xs, norm_ffn_g[i], mods, i, 3, 4)
        act = _glu(h, ffn_w_gate_up, i)
        xs = _resid(act, ffn_w_down, i, xs, mods, i, 5, name="ffn_down")
    return _final_norm(xs, final_norm_g)
```
